```python
import math
import jax, jax.numpy as jnp
from jax import lax
import numpy as np

D_MODEL = 1024
BATCH = 4
SEQ = 4096
DEPTH = 1
DEC_BATCH = 128
DEC_SEQ = 4
PAST_LEN = 2048
PAGE_SIZE = 128

ATT_HEADS = 4
QK_DIM = D_MODEL // (4 * ATT_HEADS)
QK2 = 2 * QK_DIM
V_DIM = 2 * QK_DIM
Q_W = ATT_HEADS * QK2
ATT_W = ATT_HEADS * V_DIM
POOL_WINDOWS = (2, 4, 8, 16)
POOL_GROUPS = len(POOL_WINDOWS)
POOL_GW = D_MODEL // (2 * POOL_GROUPS)
POOL_W = POOL_GROUPS * POOL_GW
POOL_PAD = max(POOL_WINDOWS) - 1
MIX_W = ATT_W + POOL_W
IN_W = 2 * Q_W + ATT_W + POOL_W
REL_BUCKETS = 32
REL_MAX_DIST = 128
Q_BLOCK = 128
PEER_HEADS = 8
PEER_KEYS = 128
PEER_EXPERTS = PEER_KEYS * PEER_KEYS
PEER_QDIM = 256
PEER_HALF = PEER_QDIM // 2
PEER_TOPK = 16
PEER_BLOCK = 128
PLE_DIM = 256
NEG_INF = -1e30
EPS = 1e-6

kernel_name = 'hymba_diffattn_pool_peer_decode_step'


def _rmsnorm(x, g):
    xf = x.astype(jnp.float32)
    y = xf * lax.rsqrt(jnp.mean(xf * xf, axis=-1, keepdims=True) + EPS)
    return (y * g.astype(jnp.float32)).astype(x.dtype)


def _t5_bucket(rel):
    n = jnp.maximum(rel, 0)
    exact = REL_BUCKETS // 2
    large = exact + (jnp.log(jnp.maximum(n, 1).astype(jnp.float32) / exact)
                     / math.log(REL_MAX_DIST / exact) * (REL_BUCKETS - exact)).astype(jnp.int32)
    large = jnp.minimum(large, REL_BUCKETS - 1)
    return jnp.where(n < exact, n, large)


def _diff_attn_block(q, k, v, q_pos, k_pos, lam, rel_bias):
    scale = QK_DIM ** -0.5
    bias = rel_bias[_t5_bucket(q_pos[:, None] - k_pos[None, :])]
    bias = jnp.transpose(bias, (2, 0, 1))[None].astype(jnp.float32)
    mask = (k_pos[None, :] <= q_pos[:, None])[None, None]

    def probs(qa, ka):
        s = jnp.einsum('bqhd,bkhd->bhqk', qa, ka).astype(jnp.float32) * scale + bias
        return jax.nn.softmax(jnp.where(mask, s, NEG_INF), axis=-1)

    a = probs(q[..., :QK_DIM], k[..., :QK_DIM]) - lam * probs(q[..., QK_DIM:], k[..., QK_DIM:])
    return jnp.einsum('bhqk,bkhd->bqhd', a.astype(v.dtype), v)


def _diff_attention(q, k, v, q_pos, k_pos, lam, rel_bias):
    B, L = q.shape[0], q.shape[1]
    if L > Q_BLOCK and L % Q_BLOCK == 0:
        nb = L // Q_BLOCK
        qb = jnp.moveaxis(q.reshape(B, nb, Q_BLOCK, ATT_HEADS, QK2), 1, 0)
        pb = q_pos.reshape(nb, Q_BLOCK)
        out = lax.map(lambda a: _diff_attn_block(a[0], k, v, a[1], k_pos, lam, rel_bias), (qb, pb))
        return jnp.moveaxis(out, 0, 1).reshape(B, L, ATT_HEADS, V_DIM)
    return _diff_attn_block(q, k, v, q_pos, k_pos, lam, rel_bias)


def _pool_mixer(u, prev, start_pos, pool_w, pool_scale):
    L = u.shape[1]
    z = jnp.concatenate([prev, u], axis=1)
    zf = z.astype(jnp.float32)
    cs = jnp.concatenate([jnp.zeros_like(zf[:, :1]), jnp.cumsum(zf, axis=1)], axis=1)
    pos = start_pos + jnp.arange(L, dtype=jnp.int32)
    outs = []
    for g, w in enumerate(POOL_WINDOWS):
        c0, c1 = g * POOL_GW, (g + 1) * POOL_GW
        hi = cs[:, POOL_PAD + 1: POOL_PAD + 1 + L, c0:c1]
        lo = cs[:, POOL_PAD + 1 - w: POOL_PAD + 1 - w + L, c0:c1]
        cnt = jnp.minimum(pos + 1, w).astype(jnp.float32)[None, :, None]
        d = ((hi - lo) / cnt - u[..., c0:c1].astype(jnp.float32)).astype(u.dtype)
        outs.append(d @ pool_w[g])
    out = jnp.concatenate(outs, axis=-1) * pool_scale
    return out, z[:, -POOL_PAD:, :]


def _peer_block(xb, wq, sub_keys, u_tab, v_tab):
    T = xb.shape[0]
    q = (xb @ wq).reshape(T, PEER_HEADS, 2, PEER_HALF)
    s = jnp.einsum('thpd,hpkd->thpk', q, sub_keys).astype(jnp.float32)
    s1, i1 = lax.top_k(s[:, :, 0], PEER_TOPK)
    s2, i2 = lax.top_k(s[:, :, 1], PEER_TOPK)
    cand = (s1[..., :, None] + s2[..., None, :]).reshape(T, PEER_HEADS, PEER_TOPK * PEER_TOPK)
    cidx = (i1[..., :, None] * PEER_KEYS + i2[..., None, :]).reshape(T, PEER_HEADS, PEER_TOPK * PEER_TOPK)
    top, sel = lax.top_k(cand, PEER_TOPK)
    idx = jnp.take_along_axis(cidx, sel, axis=-1)
    gate = jax.nn.softmax(top, axis=-1)
    u = u_tab[idx]
    h = jax.nn.gelu(jnp.einsum('td,thkd->thk', xb, u).astype(jnp.float32), approximate=False)
    v = v_tab[idx]
    return jnp.einsum('thk,thkd->td', (gate * h).astype(xb.dtype), v)


def _peer(xn, wq, sub_keys, u_tab, v_tab):
    T = xn.shape[0]
    nb = -(-T // PEER_BLOCK)
    xp = jnp.pad(xn, ((0, nb * PEER_BLOCK - T), (0, 0))).reshape(nb, PEER_BLOCK, D_MODEL)
    out = lax.map(lambda xb: _peer_block(xb, wq, sub_keys, u_tab, v_tab), xp)
    return out.reshape(nb * PEER_BLOCK, D_MODEL)[:T]


def _layer(x, p, past_k, past_v, pool_prev, start_pos, lam_init,
           norm_attn_g, w_in, lq1, lk1, lq2, lk2, subln_g, rel_bias, pool_w, pool_scale,
           w_out, norm_ffn_g, peer_wq, peer_keys, peer_u, peer_v, norm_pl_g, w_pl, w_pl_gate):
    B, L, _ = x.shape
    xn = _rmsnorm(x, norm_attn_g)
    proj = xn @ w_in
    q = proj[..., :Q_W].reshape(B, L, ATT_HEADS, QK2)
    k = proj[..., Q_W:2 * Q_W].reshape(B, L, ATT_HEADS, QK2)
    v = proj[..., 2 * Q_W:2 * Q_W + ATT_W].reshape(B, L, ATT_HEADS, V_DIM)
    u = proj[..., 2 * Q_W + ATT_W:]
    past_len = past_k.shape[1]
    k_all = jnp.concatenate([past_k, k], axis=1)
    v_all = jnp.concatenate([past_v, v], axis=1)
    k_pos = start_pos - past_len + jnp.arange(past_len + L, dtype=jnp.int32)
    q_pos = start_pos + jnp.arange(L, dtype=jnp.int32)
    lam = (jnp.exp(jnp.sum(lq1.astype(jnp.float32) * lk1.astype(jnp.float32)))
           - jnp.exp(jnp.sum(lq2.astype(jnp.float32) * lk2.astype(jnp.float32))) + lam_init)
    attn = _diff_attention(q, k_all, v_all, q_pos, k_pos, lam, rel_bias)
    attn = _rmsnorm(attn, subln_g) * (1.0 - lam_init)
    pool_out, new_pool = _pool_mixer(u, pool_prev, start_pos, pool_w, pool_scale)
    mixed = jnp.concatenate([attn.reshape(B, L, ATT_W), pool_out], axis=-1)
    x = x + mixed @ w_out
    xn2 = _rmsnorm(x, norm_ffn_g)
    x = x + _peer(xn2.reshape(B * L, D_MODEL), peer_wq, peer_keys, peer_u, peer_v).reshape(B, L, D_MODEL)
    gate = jax.nn.sigmoid(_rmsnorm(x, norm_pl_g) @ w_pl_gate)
    x = x + gate * (p @ w_pl)
    return x, k, v, new_pool


def setup_inputs(seed: int = 0) -> dict:
    key = jax.random.key(seed)
    ks = jax.random.split(key, 32)
    f32 = jnp.float32

    def nrm(k, shape, scale):
        return jax.random.normal(k, shape, f32) * scale

    n_pages = PAST_LEN // PAGE_SIZE
    n_used = DEC_BATCH * n_pages
    n_phys = n_used + max(1, n_used // 4)
    perm = jax.random.permutation(ks[0], n_phys)
    page_table = perm[:n_used].reshape(DEC_BATCH, n_pages).astype(jnp.int32)
    return {
        'x_prompt': nrm(ks[1], (BATCH, SEQ, D_MODEL), 1.0),
        'x_sample': nrm(ks[2], (DEC_BATCH, DEC_SEQ, D_MODEL), 1.0),
        'cache_k': nrm(ks[3], (DEPTH, n_phys, PAGE_SIZE, ATT_HEADS, QK2), 1.0),
        'cache_v': nrm(ks[4], (DEPTH, n_phys, PAGE_SIZE, ATT_HEADS, V_DIM), 1.0),
        'page_table': page_table,
        'state_pool': nrm(ks[5], (DEPTH, DEC_BATCH, POOL_PAD, POOL_W), 1.0),
        'p_prompt': nrm(ks[6], (DEPTH, BATCH, SEQ, PLE_DIM), 1.0),
        'p_sample': nrm(ks[7], (DEPTH, DEC_BATCH, DEC_SEQ, PLE_DIM), 1.0),
        'norm_attn_g': 1.0 + nrm(ks[8], (DEPTH, D_MODEL), 0.05),
        'w_in': nrm(ks[9], (DEPTH, D_MODEL, IN_W), D_MODEL ** -0.5),
        'lambda_q1': nrm(ks[10], (DEPTH, QK_DIM), 0.1),
        'lambda_k1': nrm(ks[11], (DEPTH, QK_DIM), 0.1),
        'lambda_q2': nrm(ks[12], (DEPTH, QK_DIM), 0.1),
        'lambda_k2': nrm(ks[13], (DEPTH, QK_DIM), 0.1),
        'subln_g': 1.0 + nrm(ks[14], (DEPTH, V_DIM), 0.05),
        'rel_bias': nrm(ks[15], (REL_BUCKETS, ATT_HEADS), 0.5),
        'pool_w': nrm(ks[16], (DEPTH, POOL_GROUPS, POOL_GW, POOL_GW), POOL_GW ** -0.5),
        'pool_scale': 1.0 + nrm(ks[17], (DEPTH, POOL_W), 0.05),
        'w_out': nrm(ks[18], (DEPTH, MIX_W, D_MODEL), MIX_W ** -0.5),
        'norm_ffn_g': 1.0 + nrm(ks[19], (DEPTH, D_MODEL), 0.05),
        'peer_wq': nrm(ks[20], (DEPTH, D_MODEL, PEER_HEADS * PEER_QDIM), D_MODEL ** -0.5),
        'peer_keys': nrm(ks[21], (DEPTH, PEER_HEADS, 2, PEER_KEYS, PEER_HALF), PEER_HALF ** -0.5),
        'peer_u': nrm(ks[22], (DEPTH, PEER_EXPERTS, D_MODEL), D_MODEL ** -0.5),
        'peer_v': nrm(ks[23], (DEPTH, PEER_EXPERTS, D_MODEL), 0.3),
        'norm_pl_g': 1.0 + nrm(ks[24], (DEPTH, D_MODEL), 0.05),
        'w_pl': nrm(ks[25], (DEPTH, PLE_DIM, D_MODEL), 0.5 * PLE_DIM ** -0.5),
        'w_pl_gate': nrm(ks[26], (DEPTH, D_MODEL, D_MODEL), D_MODEL ** -0.5),
        'final_norm_g': 1.0 + nrm(ks[27], (D_MODEL,), 0.05),
    }


def reference(x_prompt, x_sample, cache_k, cache_v, page_table, state_pool, p_prompt, p_sample,
              norm_attn_g, w_in, lambda_q1, lambda_k1, lambda_q2, lambda_k2, subln_g, rel_bias,
              pool_w, pool_scale, w_out, norm_ffn_g, peer_wq, peer_keys, peer_u, peer_v,
              norm_pl_g, w_pl, w_pl_gate, final_norm_g):
    hp, hs = x_prompt, x_sample
    kp_l, vp_l, pp_l, ks_l, vs_l, ps_l = [], [], [], [], [], []
    for i in range(DEPTH):
        lam_init = 0.8 - 0.6 * math.exp(-0.3 * i)
        wts = (norm_attn_g[i], w_in[i], lambda_q1[i], lambda_k1[i], lambda_q2[i], lambda_k2[i],
               subln_g[i], rel_bias, pool_w[i], pool_scale[i], w_out[i], norm_ffn_g[i],
               peer_wq[i], peer_keys[i], peer_u[i], peer_v[i], norm_pl_g[i], w_pl[i], w_pl_gate[i])
        empty_k = jnp.zeros((BATCH, 0, ATT_HEADS, QK2), x_prompt.dtype)
        empty_v = jnp.zeros((BATCH, 0, ATT_HEADS, V_DIM), x_prompt.dtype)
        zero_pool = jnp.zeros((BATCH, POOL_PAD, POOL_W), x_prompt.dtype)
        hp, kp, vp, pp = _layer(hp, p_prompt[i], empty_k, empty_v, zero_pool, 0, lam_init, *wts)
        past_k = cache_k[i][page_table].reshape(DEC_BATCH, -1, ATT_HEADS, QK2)
        past_v = cache_v[i][page_table].reshape(DEC_BATCH, -1, ATT_HEADS, V_DIM)
        hs, ksm, vsm, psm = _layer(hs, p_sample[i], past_k, past_v, state_pool[i], PAST_LEN, lam_init, *wts)
        kp_l.append(kp); vp_l.append(vp); pp_l.append(pp)
        ks_l.append(ksm); vs_l.append(vsm); ps_l.append(psm)
    y_prompt = _rmsnorm(hp, final_norm_g)
    y_sample = _rmsnorm(hs, final_norm_g)
    new_k_prompt = jnp.stack(kp_l, axis=0)
    new_v_prompt = jnp.stack(vp_l, axis=0)
    new_pool_prompt = jnp.stack(pp_l, axis=0)
    new_k_sample = jnp.stack(ks_l, axis=0)
    new_v_sample = jnp.stack(vs_l, axis=0)
    new_pool_sample = jnp.stack(ps_l, axis=0)
    return (y_prompt, y_sample, new_k_prompt, new_v_prompt, new_pool_prompt, new_k_sample, new_v_sample, new_pool_sample)
```

```python
import functools
import math

import jax
import jax.numpy as jnp
import numpy as np
from jax import lax
from jax.experimental import pallas as pl
from jax.experimental.pallas import tpu as pltpu

F32 = jnp.float32
BF16 = jnp.bfloat16

EPS = 1e-6
NEG_INF = -1e30
HEADS = 4
QK_DIM = 64
HEAD_W = 2 * QK_DIM
ATT_W = HEADS * HEAD_W
POOL_WINDOWS = (2, 4, 8, 16)
POOL_GW = 128
POOL_PAD = 15
POOL_HALO = 16
REL_BUCKETS = 32
REL_MAX_DIST = 128
PEER_HEADS = 8
PEER_KEYS = 128
PEER_TOPK = 16
LANES = 128
VMEM_LIMIT = 56 * 1024 * 1024


def _bucket_thresholds():
    exact = REL_BUCKETS // 2
    n = np.arange(0, 4 * REL_MAX_DIST)
    large = exact + (np.log(np.maximum(n, 1).astype(np.float32) / np.float32(exact))
                     / np.float32(math.log(REL_MAX_DIST / exact))
                     * np.float32(REL_BUCKETS - exact)).astype(np.int32)
    large = np.minimum(large, REL_BUCKETS - 1)
    bucket = np.where(n < exact, n, large)
    return [int(n[np.argmax(bucket >= b)]) for b in range(REL_BUCKETS)]


_BUCKET_START = _bucket_thresholds()
_FAR_DIST = _BUCKET_START[REL_BUCKETS - 1]


def _rel_bias_tile(rel, rb_ref, h):
    out = jnp.full(rel.shape, rb_ref[h], F32)
    for b in range(1, REL_BUCKETS):
        out = jnp.where(rel >= _BUCKET_START[b], rb_ref[b * HEADS + h], out)
    return out


def _rmsnorm_rows(x, g):
    return x * lax.rsqrt(jnp.mean(x * x, axis=-1, keepdims=True) + EPS) * g


def _cparams(sem):
    return pltpu.CompilerParams(dimension_semantics=sem, vmem_limit_bytes=VMEM_LIMIT)


def _inproj_kernel(x_ref, g_ref, w_ref, q_ref, k_ref, v_ref, u_ref):
    xn = _rmsnorm_rows(x_ref[...], g_ref[...])
    proj = jnp.dot(xn.astype(BF16), w_ref[...], preferred_element_type=F32)
    q_ref[...] = proj[:, 0 * ATT_W:1 * ATT_W]
    k_ref[...] = proj[:, 1 * ATT_W:2 * ATT_W]
    v_ref[...] = proj[:, 2 * ATT_W:3 * ATT_W]
    u_ref[...] = proj[:, 3 * ATT_W:4 * ATT_W]


def _inproj(x, g, w_bf, tm):
    t, d = x.shape
    n_out = w_bf.shape[1]
    assert n_out == 4 * ATT_W and t % tm == 0
    out = jax.ShapeDtypeStruct((t, ATT_W), F32)
    row = pl.BlockSpec((tm, ATT_W), lambda i: (i, 0))
    return pl.pallas_call(
        _inproj_kernel,
        grid=(t // tm,),
        in_specs=[pl.BlockSpec((tm, d), lambda i: (i, 0)),
                  pl.BlockSpec((1, d), lambda i: (0, 0)),
                  pl.BlockSpec((d, n_out), lambda i: (0, 0))],
        out_specs=[row, row, row, row],
        out_shape=[out, out, out, out],
        compiler_params=_cparams(("arbitrary",)),
        name="inproj",
    )(x, g, w_bf)


def _lambda_full(lq1, lk1, lq2, lk2, lam_init):
    return (jnp.exp(jnp.sum(lq1 * lk1, axis=-1, keepdims=True))
            - jnp.exp(jnp.sum(lq2 * lk2, axis=-1, keepdims=True)) + lam_init)


def _head_out(acc1, l1, acc2, l2, lam, subln_g, lam_init):
    o = acc1 / l1 - lam * (acc2 / l2)
    return _rmsnorm_rows(o, subln_g) * (1.0 - lam_init)


def _attn_prompt_kernel(rb_ref, q_ref, k_ref, v_ref, lq1_ref, lk1_ref, lq2_ref, lk2_ref, sg_ref,
                        o_ref, bias_ref, m_ref, l_ref, acc_ref, *, tb, lam_init):
    b, qi, kj = pl.program_id(0), pl.program_id(1), pl.program_id(2)

    @pl.when((b == 0) & (qi == 0) & (kj == 0))
    def _build_bias():
        row = lax.broadcasted_iota(jnp.int32, (tb, tb), 0)
        col = lax.broadcasted_iota(jnp.int32, (tb, tb), 1)
        rel = row - col
        for h in range(HEADS):
            diag = _rel_bias_tile(rel, rb_ref, h)
            bias_ref[0, h] = jnp.where(rel >= 0, diag, NEG_INF)
            bias_ref[1, h] = _rel_bias_tile(rel + tb, rb_ref, h)
            bias_ref[2, h] = jnp.full((tb, tb), rb_ref[(REL_BUCKETS - 1) * HEADS + h], F32)

    @pl.when(kj == 0)
    def _init():
        m_ref[...] = jnp.full(m_ref.shape, -jnp.inf, F32)
        l_ref[...] = jnp.zeros(l_ref.shape, F32)
        acc_ref[...] = jnp.zeros(acc_ref.shape, F32)

    @pl.when(kj <= qi)
    def _step():
        sel = jnp.where(kj == qi, 0, jnp.where(kj == qi - 1, 1, 2))
        scale = QK_DIM ** -0.5
        for h in range(HEADS):
            bias = bias_ref[sel, h]
            vh = v_ref[:, h * HEAD_W:(h + 1) * HEAD_W].astype(BF16)
            for mp in range(2):
                c0 = h * HEAD_W + mp * QK_DIM
                qh = (q_ref[:, c0:c0 + QK_DIM] * scale).astype(BF16)
                kh = k_ref[:, c0:c0 + QK_DIM].astype(BF16)
                s = lax.dot_general(qh, kh, (((1,), (1,)), ((), ())),
                                    preferred_element_type=F32) + bias
                idx = h * 2 + mp
                m_old = m_ref[idx]
                m_new = jnp.maximum(m_old, jnp.max(s, axis=-1, keepdims=True))
                alpha = jnp.exp(m_old - m_new)
                p = jnp.exp(s - m_new)
                l_ref[idx] = alpha * l_ref[idx] + jnp.sum(p, axis=-1, keepdims=True)
                acc_ref[idx] = alpha * acc_ref[idx] + jnp.dot(
                    p.astype(BF16), vh, preferred_element_type=F32)
                m_ref[idx] = m_new

    @pl.when(kj == qi)
    def _finish():
        lam = _lambda_full(lq1_ref[...], lk1_ref[...], lq2_ref[...], lk2_ref[...], lam_init)
        for h in range(HEADS):
            o_ref[:, h * HEAD_W:(h + 1) * HEAD_W] = _head_out(
                acc_ref[2 * h], l_ref[2 * h], acc_ref[2 * h + 1], l_ref[2 * h + 1],
                lam, sg_ref[...], lam_init)


def _attn_prompt(q, k, v, rb_flat, lq1, lk1, lq2, lk2, subln_g, batch, seq, tb, lam_init):
    t = q.shape[0]
    assert t == batch * seq and seq % tb == 0 and tb >= _FAR_DIST
    nq = seq // tb
    small = lambda n: pl.BlockSpec((1, n), lambda b, i, j: (0, 0))
    kv_spec = pl.BlockSpec((tb, ATT_W), lambda b, i, j: (b * nq + jnp.minimum(j, i), 0))
    q_spec = pl.BlockSpec((tb, ATT_W), lambda b, i, j: (b * nq + i, 0))
    return pl.pallas_call(
        functools.partial(_attn_prompt_kernel, tb=tb, lam_init=lam_init),
        grid=(batch, nq, nq),
        in_specs=[pl.BlockSpec(memory_space=pltpu.SMEM),
                  q_spec, kv_spec, kv_spec,
                  small(QK_DIM), small(QK_DIM), small(QK_DIM), small(QK_DIM), small(HEAD_W)],
        out_specs=q_spec,
        out_shape=jax.ShapeDtypeStruct((t, ATT_W), F32),
        scratch_shapes=[pltpu.VMEM((3, HEADS, tb, tb), F32),
                        pltpu.VMEM((2 * HEADS, tb, 1), F32),
                        pltpu.VMEM((2 * HEADS, tb, 1), F32),
                        pltpu.VMEM((2 * HEADS, tb, HEAD_W), F32)],
        compiler_params=_cparams(("arbitrary", "arbitrary", "arbitrary")),
        name="attn_prompt",
    )(rb_flat, q, k, v, lq1, lk1, lq2, lk2, subln_g)


QROWS = 8


def _attn_sample_kernel(pt_ref, rb_ref, q_ref, kn_ref, vn_ref, kc_ref, vc_ref,
                        lq1_ref, lk1_ref, lq2_ref, lk2_ref, sg_ref,
                        o_ref, q8_ref, qbd_ref, m_ref, l_ref, acc_ref,
                        *, n_pages, page, dec, lam_init):
    del pt_ref
    p = pl.program_id(1)
    nblk = 2 * HEADS
    rows = nblk * QROWS
    past_len = n_pages * page
    row_i = lax.broadcasted_iota(jnp.int32, (rows, 1), 0) % QROWS

    @pl.when(p == 0)
    def _init():
        q8_ref[...] = jnp.zeros(q8_ref.shape, F32)
        q8_ref[0:dec, :] = q_ref[0] * (QK_DIM ** -0.5)
        q8 = q8_ref[...]
        col = lax.broadcasted_iota(jnp.int32, (QROWS, ATT_W), 1)
        for blk in range(nblk):
            c0 = blk * QK_DIM
            qbd_ref[blk * QROWS:(blk + 1) * QROWS, :] = jnp.where(
                (col >= c0) & (col < c0 + QK_DIM), q8, 0.0)
        m_ref[...] = jnp.full(m_ref.shape, -jnp.inf, F32)
        l_ref[...] = jnp.zeros(l_ref.shape, F32)
        acc_ref[...] = jnp.zeros(acc_ref.shape, F32)

    def bias_rows(rel):
        parts = [_rel_bias_tile(rel[h * 2 * QROWS:(h + 1) * 2 * QROWS], rb_ref, h)
                 for h in range(HEADS)]
        return jnp.concatenate(parts, axis=0)

    qbd = qbd_ref[...]
    s = lax.dot_general(qbd.astype(BF16), kc_ref[0].astype(BF16), (((1,), (1,)), ((), ())),
                        preferred_element_type=F32)
    kcol = lax.broadcasted_iota(jnp.int32, (rows, page), 1)
    rel = past_len + row_i - (p * page + kcol)
    s = s + bias_rows(rel)
    m_old = m_ref[...]
    m_new = jnp.maximum(m_old, jnp.max(s, axis=-1, keepdims=True))
    alpha = jnp.exp(m_old - m_new)
    pr = jnp.exp(s - m_new)
    l_ref[...] = alpha * l_ref[...] + jnp.sum(pr, axis=-1, keepdims=True)
    acc_ref[...] = alpha * acc_ref[...] + jnp.dot(pr.astype(BF16), vc_ref[0].astype(BF16),
                                                  preferred_element_type=F32)
    m_ref[...] = m_new

    @pl.when(p == n_pages - 1)
    def _finish():
        s_new = []
        for j in range(dec):
            kj = kn_ref[0, j:j + 1, :]
            sj = jnp.sum(qbd * kj, axis=-1, keepdims=True)
            sj = sj + bias_rows(row_i - j)
            s_new.append(jnp.where(row_i >= j, sj, NEG_INF))
        m_old2 = m_ref[...]
        m_fin = m_old2
        for sj in s_new:
            m_fin = jnp.maximum(m_fin, sj)
        alpha2 = jnp.exp(m_old2 - m_fin)
        l_fin = alpha2 * l_ref[...]
        acc_fin = alpha2 * acc_ref[...]
        for j, sj in enumerate(s_new):
            pj = jnp.exp(sj - m_fin)
            l_fin = l_fin + pj
            acc_fin = acc_fin + pj * vn_ref[0, j:j + 1, :]
        lam = _lambda_full(lq1_ref[...], lk1_ref[...], lq2_ref[...], lk2_ref[...], lam_init)
        for h in range(HEADS):
            r1, r2 = (2 * h) * QROWS, (2 * h + 1) * QROWS
            cs = slice(h * HEAD_W, (h + 1) * HEAD_W)
            oh = _head_out(acc_fin[r1:r1 + QROWS, cs], l_fin[r1:r1 + QROWS],
                           acc_fin[r2:r2 + QROWS, cs], l_fin[r2:r2 + QROWS],
                           lam, sg_ref[...], lam_init)
            o_ref[0, :, cs] = oh[0:dec]


def _attn_sample(q, k_new, v_new, cache_k, cache_v, page_table, rb_flat,
                 lq1, lk1, lq2, lk2, subln_g, dec, lam_init):
    nseq, n_pages = page_table.shape
    n_phys, page = cache_k.shape[0], cache_k.shape[1]
    assert dec <= QROWS and page >= _FAR_DIST
    q3 = q.reshape(nseq, dec, ATT_W)
    kn3 = k_new.reshape(nseq, dec, ATT_W)
    vn3 = v_new.reshape(nseq, dec, ATT_W)
    kc = cache_k.reshape(n_phys, page, ATT_W)
    vc = cache_v.reshape(n_phys, page, ATT_W)
    rows = 2 * HEADS * QROWS
    seq_spec = pl.BlockSpec((1, dec, ATT_W), lambda b, p, pt: (b, 0, 0))
    page_spec = pl.BlockSpec((1, page, ATT_W), lambda b, p, pt: (pt[b, p], 0, 0))
    small = lambda n: pl.BlockSpec((1, n), lambda b, p, pt: (0, 0))
    grid_spec = pltpu.PrefetchScalarGridSpec(
        num_scalar_prefetch=1,
        grid=(nseq, n_pages),
        in_specs=[pl.BlockSpec(memory_space=pltpu.SMEM),
                  seq_spec, seq_spec, seq_spec, page_spec, page_spec,
                  small(QK_DIM), small(QK_DIM), small(QK_DIM), small(QK_DIM), small(HEAD_W)],
        out_specs=seq_spec,
        scratch_shapes=[pltpu.VMEM((QROWS, ATT_W), F32),
                        pltpu.VMEM((rows, ATT_W), F32),
                        pltpu.VMEM((rows, 1), F32),
                        pltpu.VMEM((rows, 1), F32),
                        pltpu.VMEM((rows, ATT_W), F32)])
    out = pl.pallas_call(
        functools.partial(_attn_sample_kernel, n_pages=n_pages, page=page, dec=dec,
                          lam_init=lam_init),
        grid_spec=grid_spec,
        out_shape=jax.ShapeDtypeStruct((nseq, dec, ATT_W), F32),
        compiler_params=_cparams(("arbitrary", "arbitrary")),
        name="attn_sample",
    )(page_table, rb_flat, q3, kn3, vn3, kc, vc, lq1, lk1, lq2, lk2, subln_g)
    return out.reshape(nseq * dec, ATT_W)


def _pool_groups(window_sum, u_of, cnt_of, pw_ref, ps_ref):
    outs = []
    for g, w in enumerate(POOL_WINDOWS):
        d = window_sum(g, w) / cnt_of(w) - u_of(g)
        outs.append(jnp.dot(d.astype(BF16), pw_ref[g], preferred_element_type=F32))
    return jnp.concatenate(outs, axis=-1) * ps_ref[...]


def _mix_prompt_kernel(x_ref, a_ref, u_ref, halo_ref, pw_ref, ps_ref, wo_ref, o_ref, z_ref,
                       *, tm, tiles_per_seq):
    i = pl.program_id(0)
    t_in_seq = i % tiles_per_seq
    z_ref[0:POOL_HALO, :] = jnp.where(t_in_seq == 0, 0.0, halo_ref[...])
    z_ref[POOL_HALO:POOL_HALO + tm, :] = u_ref[...]
    pos = t_in_seq * tm + lax.broadcasted_iota(jnp.int32, (tm, 1), 0)

    def window_sum(g, w):
        cs = slice(g * POOL_GW, (g + 1) * POOL_GW)
        acc = z_ref[POOL_HALO:POOL_HALO + tm, cs]
        for k in range(1, w):
            acc = acc + z_ref[POOL_HALO - k:POOL_HALO - k + tm, cs]
        return acc

    pool = _pool_groups(
        window_sum,
        lambda g: u_ref[:, g * POOL_GW:(g + 1) * POOL_GW],
        lambda w: jnp.minimum(pos + 1, w).astype(F32),
        pw_ref, ps_ref)
    mixed = jnp.concatenate([a_ref[...], pool], axis=-1).astype(BF16)
    o_ref[...] = x_ref[...] + jnp.dot(mixed, wo_ref[...], preferred_element_type=F32)


def _mix_prompt(x, attn, u, pool_w_bf, pool_scale, w_out_bf, seq, tm):
    t, d = x.shape
    pw_cols = u.shape[1]
    assert seq % tm == 0 and tm % POOL_HALO == 0
    halo_blocks = tm // POOL_HALO
    return pl.pallas_call(
        functools.partial(_mix_prompt_kernel, tm=tm, tiles_per_seq=seq // tm),
        grid=(t // tm,),
        in_specs=[pl.BlockSpec((tm, d), lambda i: (i, 0)),
                  pl.BlockSpec((tm, ATT_W), lambda i: (i, 0)),
                  pl.BlockSpec((tm, pw_cols), lambda i: (i, 0)),
                  pl.BlockSpec((POOL_HALO, pw_cols),
                               lambda i: (jnp.maximum(i * halo_blocks - 1, 0), 0)),
                  pl.BlockSpec(pool_w_bf.shape, lambda i: (0, 0, 0)),
                  pl.BlockSpec((1, pw_cols), lambda i: (0, 0)),
                  pl.BlockSpec(w_out_bf.shape, lambda i: (0, 0))],
        out_specs=pl.BlockSpec((tm, d), lambda i: (i, 0)),
        out_shape=jax.ShapeDtypeStruct((t, d), F32),
        scratch_shapes=[pltpu.VMEM((POOL_HALO + tm, pw_cols), F32)],
        compiler_params=_cparams(("arbitrary",)),
        name="mix_prompt",
    )(x, attn, u, u, pool_w_bf, pool_scale, w_out_bf)


def _mix_sample_kernel(x_ref, a_ref, u_ref, st_ref, pw_ref, ps_ref, wo_ref, o_ref,
                       *, dec, start_pos):
    def z_row(r):
        return st_ref[r] if r < POOL_PAD else u_ref[r - POOL_PAD]

    for i in range(dec):
        def window_sum(g, w, i=i):
            cs = slice(g * POOL_GW, (g + 1) * POOL_GW)
            acc = z_row(POOL_PAD + i)[:, cs]
            for k in range(1, w):
                acc = acc + z_row(POOL_PAD + i - k)[:, cs]
            return acc

        pool = _pool_groups(
            window_sum,
            lambda g, i=i: u_ref[i][:, g * POOL_GW:(g + 1) * POOL_GW],
            lambda w, i=i: float(min(start_pos + i + 1, w)),
            pw_ref, ps_ref)
        mixed = jnp.concatenate([a_ref[i], pool], axis=-1).astype(BF16)
        o_ref[i] = x_ref[i] + jnp.dot(mixed, wo_ref[...], preferred_element_type=F32)


def _mix_sample(x3, attn3, u3, state3, pool_w_bf, pool_scale, w_out_bf, start_pos):
    dec, nseq, d = x3.shape
    full = lambda a: pl.BlockSpec(a.shape, lambda i: (0,) * a.ndim)
    args = (x3, attn3, u3, state3, pool_w_bf, pool_scale, w_out_bf)
    return pl.pallas_call(
        functools.partial(_mix_sample_kernel, dec=dec, start_pos=start_pos),
        grid=(1,),
        in_specs=[full(a) for a in args],
        out_specs=full(x3),
        out_shape=jax.ShapeDtypeStruct((dec, nseq, d), F32),
        compiler_params=_cparams(("arbitrary",)),
        name="mix_sample",
    )(*args)


def _erf_gelu(x):
    return 0.5 * x * (1.0 + lax.erf(x * (2.0 ** -0.5)))


def _top_rows(x, n):
    rows = []
    for r in range(n):
        m = jnp.max(x, axis=0, keepdims=True)
        rows.append(m)
        if r + 1 < n:
            x = jnp.where(x == m, -jnp.inf, x)
    return rows


def _stack_rows(rows):
    n = len(rows)
    rid = lax.broadcasted_iota(jnp.int32, (n, LANES), 0)
    out = jnp.broadcast_to(rows[0], (n, LANES))
    for r in range(1, n):
        out = jnp.where(rid == r, rows[r], out)
    return out


def _peer_route_block(s1, s2):
    k = PEER_TOPK
    t1 = _top_rows(s1, k)
    t2 = _top_rows(s2, k)
    t1s = _stack_rows(t1)
    t2s = _stack_rows(t2)
    half = k // 2
    rid = lax.broadcasted_iota(jnp.int32, (half, LANES), 0)
    cands = [t1s[0:half] + t2[0], t1s[half:k] + t2[0], t1s[0:half] + t2[1]]
    for q in range(2, half):
        cands.append(jnp.where(rid < k // (q + 1), t1s[0:half] + t2[q], -jnp.inf))
    cands.append(t2s[half:k] + t1[0])
    cand = jnp.concatenate(cands, axis=0)
    tau = _top_rows(cand, k)[k - 1]
    top = t1[0] + t2[0]
    z = jnp.sum(jnp.where(cand >= tau, jnp.exp(cand - top), 0.0), axis=0, keepdims=True)
    thr = jnp.full(s1.shape, jnp.inf, F32)
    for q in range(k):
        thr = jnp.where(s1 + t2[q] >= tau, t2[q], thr)
    a = jnp.exp(s1 - t1[0])
    b = jnp.exp(s2 - t2[0]) / z
    return thr, a, b


def _peer_kernel(x_ref, g_ref, wq_ref, keys_ref, u_ref, vt_ref, o_ref,
                 xn_ref, s1_ref, s2_ref, thr_ref, a_ref, b_ref, pt_ref, acc_ref,
                 *, tt, ec):
    j = pl.program_id(1)
    nsb = tt // LANES
    c_per = ec // PEER_KEYS

    @pl.when(j == 0)
    def _route():
        xn = _rmsnorm_rows(x_ref[...], g_ref[...]).astype(BF16)
        xn_ref[...] = xn

        def head_scores(h, carry):
            qh = jnp.dot(xn_ref[...], wq_ref[h], preferred_element_type=F32)
            half = qh.shape[1] // 2
            for part, dst in ((0, s1_ref), (1, s2_ref)):
                qp = qh[:, part * half:(part + 1) * half].astype(BF16)
                st = lax.dot_general(keys_ref[h, part], qp, (((1,), (1,)), ((), ())),
                                     preferred_element_type=F32)
                for sb in range(nsb):
                    dst[h, sb] = st[:, sb * LANES:(sb + 1) * LANES]
            return carry

        lax.fori_loop(0, PEER_HEADS, head_scores, 0)

        def route(idx, carry):
            h, sb = idx // nsb, idx % nsb
            thr, a, b = _peer_route_block(s1_ref[h, sb], s2_ref[h, sb])
            thr_ref[h, sb] = thr
            a_ref[h, sb] = a
            b_ref[h, sb] = b
            return carry

        lax.fori_loop(0, PEER_HEADS * nsb, route, 0)
        acc_ref[...] = jnp.zeros(acc_ref.shape, F32)

    at = lax.dot_general(u_ref[...], xn_ref[...], (((1,), (1,)), ((), ())),
                         preferred_element_type=F32)

    def gates(cl, carry):
        c = j * c_per + cl
        r0 = pl.multiple_of(cl * PEER_KEYS, PEER_KEYS)
        for sb in range(nsb):
            w = jnp.zeros((PEER_KEYS, LANES), F32)
            for h in range(PEER_HEADS):
                sel = s2_ref[h, sb] >= thr_ref[h, sb, pl.ds(c, 1), :]
                w = w + jnp.where(sel, b_ref[h, sb] * a_ref[h, sb, pl.ds(c, 1), :], 0.0)
            pt_ref[pl.ds(r0, PEER_KEYS), sb * LANES:(sb + 1) * LANES] = w
        return carry

    lax.fori_loop(0, c_per, gates, 0)
    p = (pt_ref[...] * _erf_gelu(at)).astype(BF16)
    acc_ref[...] += jnp.dot(vt_ref[...], p, preferred_element_type=F32)

    @pl.when(j == pl.num_programs(1) - 1)
    def _finish():
        o_ref[...] = x_ref[...] + acc_ref[...].T


def _peer(x, g, wq_heads_bf, keys_bf, u_bf, vt_bf, tt, ec):
    t, d = x.shape
    n_exp = u_bf.shape[0]
    assert t % tt == 0 and n_exp % ec == 0 and tt % LANES == 0 and ec % PEER_KEYS == 0
    assert n_exp == PEER_KEYS * PEER_KEYS
    nsb = tt // LANES
    blk = lambda: pltpu.VMEM((PEER_HEADS, nsb, PEER_KEYS, LANES), F32)
    return pl.pallas_call(
        functools.partial(_peer_kernel, tt=tt, ec=ec),
        grid=(t // tt, n_exp // ec),
        in_specs=[pl.BlockSpec((tt, d), lambda i, j: (i, 0)),
                  pl.BlockSpec((1, d), lambda i, j: (0, 0)),
                  pl.BlockSpec(wq_heads_bf.shape, lambda i, j: (0, 0, 0)),
                  pl.BlockSpec(keys_bf.shape, lambda i, j: (0, 0, 0, 0)),
                  pl.BlockSpec((ec, d), lambda i, j: (j, 0)),
                  pl.BlockSpec((d, ec), lambda i, j: (0, j))],
        out_specs=pl.BlockSpec((tt, d), lambda i, j: (i, 0)),
        out_shape=jax.ShapeDtypeStruct((t, d), F32),
        scratch_shapes=[pltpu.VMEM((tt, d), BF16),
                        blk(), blk(), blk(), blk(), blk(),
                        pltpu.VMEM((ec, tt), F32),
                        pltpu.VMEM((d, tt), F32)],
        compiler_params=_cparams(("arbitrary", "arbitrary")),
        name="peer",
    )(x, g, wq_heads_bf, keys_bf, u_bf, vt_bf)


def _tail_kernel(x_ref, p_ref, gpl_ref, wg_ref, wpl_ref, gf_ref, o_ref):
    x = x_ref[...]
    xn = _rmsnorm_rows(x, gpl_ref[...]).astype(BF16)
    gate = jax.nn.sigmoid(jnp.dot(xn, wg_ref[...], preferred_element_type=F32))
    emb = jnp.dot(p_ref[...].astype(BF16), wpl_ref[...], preferred_element_type=F32)
    o_ref[...] = _rmsnorm_rows(x + gate * emb, gf_ref[...])


def _tail(x, p, g_pl, w_gate_bf, w_pl_bf, g_final, tm):
    t, d = x.shape
    pd = p.shape[1]
    return pl.pallas_call(
        _tail_kernel,
        grid=(t // tm,),
        in_specs=[pl.BlockSpec((tm, d), lambda i: (i, 0)),
                  pl.BlockSpec((tm, pd), lambda i: (i, 0)),
                  pl.BlockSpec((1, d), lambda i: (0, 0)),
                  pl.BlockSpec((d, d), lambda i: (0, 0)),
                  pl.BlockSpec((pd, d), lambda i: (0, 0)),
                  pl.BlockSpec((1, d), lambda i: (0, 0))],
        out_specs=pl.BlockSpec((tm, d), lambda i: (i, 0)),
        out_shape=jax.ShapeDtypeStruct((t, d), F32),
        compiler_params=_cparams(("arbitrary",)),
        name="tail",
    )(x, p, g_pl, w_gate_bf, w_pl_bf, g_final)


def _row_tile(t, want):
    tm = min(t, want)
    assert t % tm == 0
    return tm


def kernel(x_prompt, x_sample, cache_k, cache_v, page_table, state_pool, p_prompt, p_sample, norm_attn_g, w_in, lambda_q1, lambda_k1, lambda_q2, lambda_k2, subln_g, rel_bias, pool_w, pool_scale, w_out, norm_ffn_g, peer_wq, peer_keys, peer_u, peer_v, norm_pl_g, w_pl, w_pl_gate, final_norm_g):
    depth = w_in.shape[0]
    assert depth == 1
    batch, seq, d = x_prompt.shape
    nseq, dec, _ = x_sample.shape
    n_pages = page_table.shape[1]
    page = cache_k.shape[2]
    past_len = n_pages * page
    lam_init = 0.8 - 0.6 * math.exp(-0.3 * 0)
    i = 0

    row = lambda a: a.reshape(1, -1)
    w_in_bf = w_in[i].astype(BF16)
    pool_w_bf = pool_w[i].astype(BF16)
    w_out_bf = w_out[i].astype(BF16)
    qd = peer_wq.shape[2] // PEER_HEADS
    wq_heads_bf = peer_wq[i].reshape(d, PEER_HEADS, qd).transpose(1, 0, 2).astype(BF16)
    keys_bf = peer_keys[i].astype(BF16)
    u_bf = peer_u[i].astype(BF16)
    vt_bf = peer_v[i].T.astype(BF16)
    w_gate_bf = w_pl_gate[i].astype(BF16)
    w_pl_bf = w_pl[i].astype(BF16)
    rb_flat = rel_bias.reshape(-1)
    lam_rows = (row(lambda_q1[i]), row(lambda_k1[i]), row(lambda_q2[i]), row(lambda_k2[i]))
    sg = row(subln_g[i])

    tp = batch * seq
    xp = x_prompt.reshape(tp, d)
    q, k, v, u = _inproj(xp, row(norm_attn_g[i]), w_in_bf, _row_tile(tp, 512))
    attn = _attn_prompt(q, k, v, rb_flat, *lam_rows, sg, batch, seq, min(seq, 512), lam_init)
    x1 = _mix_prompt(xp, attn, u, pool_w_bf, row(pool_scale[i]), w_out_bf, seq, min(seq, 512))
    x2 = _peer(x1, row(norm_ffn_g[i]), wq_heads_bf, keys_bf, u_bf, vt_bf,
               _row_tile(tp, 512), min(u_bf.shape[0], 1024))
    y_prompt = _tail(x2, p_prompt[i].reshape(tp, -1), row(norm_pl_g[i]), w_gate_bf, w_pl_bf,
                     row(final_norm_g), _row_tile(tp, 512)).reshape(batch, seq, d)
    new_k_prompt = k.reshape(1, batch, seq, HEADS, HEAD_W)
    new_v_prompt = v.reshape(1, batch, seq, HEADS, HEAD_W)
    new_pool_prompt = u.reshape(batch, seq, -1)[:, seq - POOL_PAD:, :][None]

    ts = nseq * dec
    xs = x_sample.reshape(ts, d)
    qs, ks, vs, us = _inproj(xs, row(norm_attn_g[i]), w_in_bf, _row_tile(ts, 512))
    attn_s = _attn_sample(qs, ks, vs, cache_k[i], cache_v[i], page_table, rb_flat,
                          *lam_rows, sg, dec, lam_init)
    tok_major = lambda a: a.reshape(nseq, dec, -1).transpose(1, 0, 2)
    x1s = _mix_sample(tok_major(xs), tok_major(attn_s), tok_major(us),
                      state_pool[i].transpose(1, 0, 2), pool_w_bf, row(pool_scale[i]),
                      w_out_bf, past_len)
    x1s = x1s.transpose(1, 0, 2).reshape(ts, d)
    x2s = _peer(x1s, row(norm_ffn_g[i]), wq_heads_bf, keys_bf, u_bf, vt_bf,
                _row_tile(ts, 512), min(u_bf.shape[0], 1024))
    y_sample = _tail(x2s, p_sample[i].reshape(ts, -1), row(norm_pl_g[i]), w_gate_bf, w_pl_bf,
                     row(final_norm_g), _row_tile(ts, 512)).reshape(nseq, dec, d)
    new_k_sample = ks.reshape(1, nseq, dec, HEADS, HEAD_W)
    new_v_sample = vs.reshape(1, nseq, dec, HEADS, HEAD_W)
    new_pool_sample = jnp.concatenate(
        [state_pool[i][:, dec:, :], us.reshape(nseq, dec, -1)], axis=1)[None]

    return (y_prompt, y_sample, new_k_prompt, new_v_prompt, new_pool_prompt,
            new_k_sample, new_v_sample, new_pool_sample)
```

```python
import functools
import math

import jax
import jax.numpy as jnp
import numpy as np
from jax import lax
from jax.experimental import pallas as pl
from jax.experimental.pallas import tpu as pltpu

F32 = jnp.float32
BF16 = jnp.bfloat16

EPS = 1e-6
NEG_INF = -1e30
HEADS = 4
QK_DIM = 64
HEAD_W = 2 * QK_DIM
ATT_W = HEADS * HEAD_W
POOL_WINDOWS = (2, 4, 8, 16)
POOL_GW = 128
POOL_PAD = 15
POOL_HALO = 16
REL_BUCKETS = 32
REL_MAX_DIST = 128
PEER_HEADS = 8
PEER_KEYS = 128
PEER_TOPK = 16
LANES = 128
VMEM_LIMIT = 56 * 1024 * 1024


def _bucket_thresholds():
    exact = REL_BUCKETS // 2
    n = np.arange(0, 4 * REL_MAX_DIST)
    large = exact + (np.log(np.maximum(n, 1).astype(np.float32) / np.float32(exact))
                     / np.float32(math.log(REL_MAX_DIST / exact))
                     * np.float32(REL_BUCKETS - exact)).astype(np.int32)
    large = np.minimum(large, REL_BUCKETS - 1)
    bucket = np.where(n < exact, n, large)
    return [int(n[np.argmax(bucket >= b)]) for b in range(REL_BUCKETS)]


_BUCKET_START = _bucket_thresholds()
_FAR_DIST = _BUCKET_START[REL_BUCKETS - 1]


def _rel_bias_tile(rel, rb_ref, h):
    out = jnp.full(rel.shape, rb_ref[h], F32)
    for b in range(1, REL_BUCKETS):
        out = jnp.where(rel >= _BUCKET_START[b], rb_ref[b * HEADS + h], out)
    return out


def _rmsnorm_rows(x, g):
    return x * lax.rsqrt(jnp.mean(x * x, axis=-1, keepdims=True) + EPS) * g


def _cparams(sem):
    return pltpu.CompilerParams(dimension_semantics=sem, vmem_limit_bytes=VMEM_LIMIT)


def _inproj_kernel(x_ref, g_ref, w_ref, q_ref, k_ref, v_ref, u_ref):
    xn = _rmsnorm_rows(x_ref[...], g_ref[...])
    proj = jnp.dot(xn.astype(BF16), w_ref[...], preferred_element_type=F32)
    q_ref[...] = proj[:, 0 * ATT_W:1 * ATT_W]
    k_ref[...] = proj[:, 1 * ATT_W:2 * ATT_W]
    v_ref[...] = proj[:, 2 * ATT_W:3 * ATT_W]
    u_ref[...] = proj[:, 3 * ATT_W:4 * ATT_W]


def _inproj(x, g, w_bf, tm):
    t, d = x.shape
    n_out = w_bf.shape[1]
    assert n_out == 4 * ATT_W and t % tm == 0
    out = jax.ShapeDtypeStruct((t, ATT_W), F32)
    row = pl.BlockSpec((tm, ATT_W), lambda i: (i, 0))
    return pl.pallas_call(
        _inproj_kernel,
        grid=(t // tm,),
        in_specs=[pl.BlockSpec((tm, d), lambda i: (i, 0)),
                  pl.BlockSpec((1, d), lambda i: (0, 0)),
                  pl.BlockSpec((d, n_out), lambda i: (0, 0))],
        out_specs=[row, row, row, row],
        out_shape=[out, out, out, out],
        compiler_params=_cparams(("arbitrary",)),
        name="inproj",
    )(x, g, w_bf)


def _lambda_full(lq1, lk1, lq2, lk2, lam_init):
    return (jnp.exp(jnp.sum(lq1 * lk1, axis=-1, keepdims=True))
            - jnp.exp(jnp.sum(lq2 * lk2, axis=-1, keepdims=True)) + lam_init)


def _head_out(acc1, l1, acc2, l2, lam, subln_g, lam_init):
    o = acc1 / l1 - lam * (acc2 / l2)
    return _rmsnorm_rows(o, subln_g) * (1.0 - lam_init)


def _attn_prompt_kernel(rb_ref, q_ref, k_ref, v_ref, lq1_ref, lk1_ref, lq2_ref, lk2_ref, sg_ref,
                        o_ref, bias_ref, m_ref, l_ref, acc_ref, *, tb, lam_init):
    b, qi, kj = pl.program_id(0), pl.program_id(1), pl.program_id(2)

    @pl.when((b == 0) & (qi == 0) & (kj == 0))
    def _build_bias():
        row = lax.broadcasted_iota(jnp.int32, (tb, tb), 0)
        col = lax.broadcasted_iota(jnp.int32, (tb, tb), 1)
        rel = row - col
        for h in range(HEADS):
            diag = _rel_bias_tile(rel, rb_ref, h)
            bias_ref[0, h] = jnp.where(rel >= 0, diag, NEG_INF)
            bias_ref[1, h] = _rel_bias_tile(rel + tb, rb_ref, h)
            bias_ref[2, h] = jnp.full((tb, tb), rb_ref[(REL_BUCKETS - 1) * HEADS + h], F32)

    @pl.when(kj == 0)
    def _init():
        m_ref[...] = jnp.full(m_ref.shape, -jnp.inf, F32)
        l_ref[...] = jnp.zeros(l_ref.shape, F32)
        acc_ref[...] = jnp.zeros(acc_ref.shape, F32)

    @pl.when(kj <= qi)
    def _step():
        sel = jnp.where(kj == qi, 0, jnp.where(kj == qi - 1, 1, 2))
        scale = QK_DIM ** -0.5
        for h in range(HEADS):
            bias = bias_ref[sel, h]
            vh = v_ref[:, h * HEAD_W:(h + 1) * HEAD_W].astype(BF16)
            for mp in range(2):
                c0 = h * HEAD_W + mp * QK_DIM
                qh = (q_ref[:, c0:c0 + QK_DIM] * scale).astype(BF16)
                kh = k_ref[:, c0:c0 + QK_DIM].astype(BF16)
                s = lax.dot_general(qh, kh, (((1,), (1,)), ((), ())),
                                    preferred_element_type=F32) + bias
                idx = h * 2 + mp
                m_old = m_ref[idx]
                m_new = jnp.maximum(m_old, jnp.max(s, axis=-1, keepdims=True))
                alpha = jnp.exp(m_old - m_new)
                p = jnp.exp(s - m_new)
                l_ref[idx] = alpha * l_ref[idx] + jnp.sum(p, axis=-1, keepdims=True)
                acc_ref[idx] = alpha * acc_ref[idx] + jnp.dot(
                    p.astype(BF16), vh, preferred_element_type=F32)
                m_ref[idx] = m_new

    @pl.when(kj == qi)
    def _finish():
        lam = _lambda_full(lq1_ref[...], lk1_ref[...], lq2_ref[...], lk2_ref[...], lam_init)
        for h in range(HEADS):
            o_ref[:, h * HEAD_W:(h + 1) * HEAD_W] = _head_out(
                acc_ref[2 * h], l_ref[2 * h], acc_ref[2 * h + 1], l_ref[2 * h + 1],
                lam, sg_ref[...], lam_init)


def _attn_prompt(q, k, v, rb_flat, lq1, lk1, lq2, lk2, subln_g, batch, seq, tb, lam_init):
    t = q.shape[0]
    assert t == batch * seq and seq % tb == 0 and tb >= _FAR_DIST
    nq = seq // tb
    small = lambda n: pl.BlockSpec((1, n), lambda b, i, j: (0, 0))
    kv_spec = pl.BlockSpec((tb, ATT_W), lambda b, i, j: (b * nq + jnp.minimum(j, i), 0))
    q_spec = pl.BlockSpec((tb, ATT_W), lambda b, i, j: (b * nq + i, 0))
    return pl.pallas_call(
        functools.partial(_attn_prompt_kernel, tb=tb, lam_init=lam_init),
        grid=(batch, nq, nq),
        in_specs=[pl.BlockSpec(memory_space=pltpu.SMEM),
                  q_spec, kv_spec, kv_spec,
                  small(QK_DIM), small(QK_DIM), small(QK_DIM), small(QK_DIM), small(HEAD_W)],
        out_specs=q_spec,
        out_shape=jax.ShapeDtypeStruct((t, ATT_W), F32),
        scratch_shapes=[pltpu.VMEM((3, HEADS, tb, tb), F32),
                        pltpu.VMEM((2 * HEADS, tb, 1), F32),
                        pltpu.VMEM((2 * HEADS, tb, 1), F32),
                        pltpu.VMEM((2 * HEADS, tb, HEAD_W), F32)],
        compiler_params=_cparams(("arbitrary", "arbitrary", "arbitrary")),
        name="attn_prompt",
    )(rb_flat, q, k, v, lq1, lk1, lq2, lk2, subln_g)


QROWS = 8


def _attn_sample_kernel(pt_ref, rb_ref, q_ref, kn_ref, vn_ref, *rest,
                        n_steps, pp, page, dec, lam_init):
    del pt_ref
    k_refs, v_refs = rest[:pp], rest[pp:2 * pp]
    (lq1_ref, lk1_ref, lq2_ref, lk2_ref, sg_ref, o_ref,
     q8_ref, qcat_ref, bias_ref, m_ref, l_ref, acc_ref) = rest[2 * pp:]
    b, p = pl.program_id(0), pl.program_id(1)
    hrows = 2 * QROWS
    rows = HEADS * hrows
    prow = page * HEADS
    width = pp * prow
    past_len = n_steps * pp * page
    row_i = lax.broadcasted_iota(jnp.int32, (rows, 1), 0) % QROWS

    def per_head(fn):
        return jnp.concatenate([fn(h) for h in range(HEADS)], axis=0)

    @pl.when(b == 0)
    def _build_bias():
        col = lax.broadcasted_iota(jnp.int32, (hrows, width), 1)
        k_pos = p * (pp * page) + col // HEADS
        rel = past_len + lax.broadcasted_iota(jnp.int32, (hrows, width), 0) % QROWS - k_pos
        bias_ref[p] = per_head(lambda h: jnp.where(col % HEADS == h,
                                                   _rel_bias_tile(rel, rb_ref, h), NEG_INF))

    @pl.when(p == 0)
    def _init():
        q8_ref[...] = jnp.zeros(q8_ref.shape, F32)
        q8_ref[0:dec, :] = q_ref[0] * (QK_DIM ** -0.5)
        col = lax.broadcasted_iota(jnp.int32, (QROWS, HEAD_W), 1)
        for h in range(HEADS):
            qh = q8_ref[:, h * HEAD_W:(h + 1) * HEAD_W]
            for mp in range(2):
                r0 = h * hrows + mp * QROWS
                qcat_ref[r0:r0 + QROWS, :] = jnp.where(col // QK_DIM == mp, qh, 0.0)
        m_ref[...] = jnp.full(m_ref.shape, -jnp.inf, F32)
        l_ref[...] = jnp.zeros(l_ref.shape, F32)
        acc_ref[...] = jnp.zeros(acc_ref.shape, F32)

    qcat = qcat_ref[...]
    qcat_bf = qcat.astype(BF16)
    s = jnp.concatenate(
        [lax.dot_general(qcat_bf, k_refs[r][...].astype(BF16), (((1,), (1,)), ((), ())),
                         preferred_element_type=F32) for r in range(pp)], axis=1)
    s = s + bias_ref[p]
    m_old = m_ref[...]
    m_new = jnp.maximum(m_old, jnp.max(s, axis=-1, keepdims=True))
    alpha = jnp.exp(m_old - m_new)
    pr = jnp.exp(s - m_new)
    l_ref[...] = alpha * l_ref[...] + jnp.sum(pr, axis=-1, keepdims=True)
    pv = jnp.dot(pr[:, 0:prow].astype(BF16), v_refs[0][...].astype(BF16),
                 preferred_element_type=F32)
    for r in range(1, pp):
        pv = pv + jnp.dot(pr[:, r * prow:(r + 1) * prow].astype(BF16),
                          v_refs[r][...].astype(BF16), preferred_element_type=F32)
    acc_ref[...] = alpha * acc_ref[...] + pv
    m_ref[...] = m_new

    @pl.when(p == n_steps - 1)
    def _finish():
        def head_row(ref, j, h):
            return ref[0, j:j + 1, h * HEAD_W:(h + 1) * HEAD_W]

        s_new = []
        for j in range(dec):
            sj = per_head(lambda h, j=j: jnp.sum(
                qcat[h * hrows:(h + 1) * hrows] * head_row(kn_ref, j, h), axis=-1, keepdims=True)
                + _rel_bias_tile(row_i[0:hrows] - j, rb_ref, h))
            s_new.append(jnp.where(row_i >= j, sj, NEG_INF))
        m_old2 = m_ref[...]
        m_fin = m_old2
        for sj in s_new:
            m_fin = jnp.maximum(m_fin, sj)
        alpha2 = jnp.exp(m_old2 - m_fin)
        l_fin = alpha2 * l_ref[...]
        acc_fin = alpha2 * acc_ref[...]
        for j, sj in enumerate(s_new):
            pj = jnp.exp(sj - m_fin)
            l_fin = l_fin + pj
            acc_fin = acc_fin + pj * per_head(
                lambda h, j=j: jnp.broadcast_to(head_row(vn_ref, j, h), (hrows, HEAD_W)))
        lam = _lambda_full(lq1_ref[...], lk1_ref[...], lq2_ref[...], lk2_ref[...], lam_init)
        for h in range(HEADS):
            r1, r2 = h * hrows, h * hrows + QROWS
            oh = _head_out(acc_fin[r1:r1 + QROWS], l_fin[r1:r1 + QROWS],
                           acc_fin[r2:r2 + QROWS], l_fin[r2:r2 + QROWS],
                           lam, sg_ref[...], lam_init)
            o_ref[0, :, h * HEAD_W:(h + 1) * HEAD_W] = oh[0:dec]


def _attn_sample(q, k_new, v_new, cache_k, cache_v, layer, page_table, rb_flat,
                 lq1, lk1, lq2, lk2, subln_g, dec, lam_init, pp):
    nseq, n_pages = page_table.shape
    depth, n_phys, page = cache_k.shape[:3]
    assert dec <= QROWS and page >= _FAR_DIST and n_pages % pp == 0
    n_steps = n_pages // pp
    q3 = q.reshape(nseq, dec, ATT_W)
    kn3 = k_new.reshape(nseq, dec, ATT_W)
    vn3 = v_new.reshape(nseq, dec, ATT_W)
    prow = page * HEADS
    kc = cache_k.reshape(depth * n_phys * prow, HEAD_W)
    vc = cache_v.reshape(depth * n_phys * prow, HEAD_W)
    rows = 2 * HEADS * QROWS
    seq_spec = pl.BlockSpec((1, dec, ATT_W), lambda b, p, pt: (b, 0, 0))

    def page_spec(r):
        return pl.BlockSpec((prow, HEAD_W),
                            lambda b, p, pt: (layer * n_phys + pt[b, p * pp + r], 0))

    pages = [page_spec(r) for r in range(pp)]
    small = lambda n: pl.BlockSpec((1, n), lambda b, p, pt: (0, 0))
    grid_spec = pltpu.PrefetchScalarGridSpec(
        num_scalar_prefetch=1,
        grid=(nseq, n_steps),
        in_specs=[pl.BlockSpec(memory_space=pltpu.SMEM), seq_spec, seq_spec, seq_spec]
        + pages + pages
        + [small(QK_DIM), small(QK_DIM), small(QK_DIM), small(QK_DIM), small(HEAD_W)],
        out_specs=seq_spec,
        scratch_shapes=[pltpu.VMEM((QROWS, ATT_W), F32),
                        pltpu.VMEM((rows, HEAD_W), F32),
                        pltpu.VMEM((n_steps, rows, pp * prow), F32),
                        pltpu.VMEM((rows, 1), F32),
                        pltpu.VMEM((rows, 1), F32),
                        pltpu.VMEM((rows, HEAD_W), F32)])
    out = pl.pallas_call(
        functools.partial(_attn_sample_kernel, n_steps=n_steps, pp=pp, page=page, dec=dec,
                          lam_init=lam_init),
        grid_spec=grid_spec,
        out_shape=jax.ShapeDtypeStruct((nseq, dec, ATT_W), F32),
        compiler_params=_cparams(("arbitrary", "arbitrary")),
        name="attn_sample",
    )(page_table, rb_flat, q3, kn3, vn3, *([kc] * pp), *([vc] * pp), lq1, lk1, lq2, lk2, subln_g)
    return out.reshape(nseq * dec, ATT_W)


def _pool_groups(window_sum, u_of, cnt_of, pw_ref, ps_ref):
    outs = []
    for g, w in enumerate(POOL_WINDOWS):
        d = window_sum(g, w) / cnt_of(w) - u_of(g)
        outs.append(jnp.dot(d.astype(BF16), pw_ref[g], preferred_element_type=F32))
    return jnp.concatenate(outs, axis=-1) * ps_ref[...]


def _mix_prompt_kernel(x_ref, a_ref, u_ref, halo_ref, pw_ref, ps_ref, wo_ref, o_ref, z_ref,
                       *, tm, tiles_per_seq):
    i = pl.program_id(0)
    t_in_seq = i % tiles_per_seq
    z_ref[0:POOL_HALO, :] = jnp.where(t_in_seq == 0, 0.0, halo_ref[...])
    z_ref[POOL_HALO:POOL_HALO + tm, :] = u_ref[...]
    pos = t_in_seq * tm + lax.broadcasted_iota(jnp.int32, (tm, 1), 0)

    def window_sum(g, w):
        cs = slice(g * POOL_GW, (g + 1) * POOL_GW)
        acc = z_ref[POOL_HALO:POOL_HALO + tm, cs]
        for k in range(1, w):
            acc = acc + z_ref[POOL_HALO - k:POOL_HALO - k + tm, cs]
        return acc

    pool = _pool_groups(
        window_sum,
        lambda g: u_ref[:, g * POOL_GW:(g + 1) * POOL_GW],
        lambda w: jnp.minimum(pos + 1, w).astype(F32),
        pw_ref, ps_ref)
    mixed = jnp.concatenate([a_ref[...], pool], axis=-1).astype(BF16)
    o_ref[...] = x_ref[...] + jnp.dot(mixed, wo_ref[...], preferred_element_type=F32)


def _mix_prompt(x, attn, u, pool_w_bf, pool_scale, w_out_bf, seq, tm):
    t, d = x.shape
    pw_cols = u.shape[1]
    assert seq % tm == 0 and tm % POOL_HALO == 0
    halo_blocks = tm // POOL_HALO
    return pl.pallas_call(
        functools.partial(_mix_prompt_kernel, tm=tm, tiles_per_seq=seq // tm),
        grid=(t // tm,),
        in_specs=[pl.BlockSpec((tm, d), lambda i: (i, 0)),
                  pl.BlockSpec((tm, ATT_W), lambda i: (i, 0)),
                  pl.BlockSpec((tm, pw_cols), lambda i: (i, 0)),
                  pl.BlockSpec((POOL_HALO, pw_cols),
                               lambda i: (jnp.maximum(i * halo_blocks - 1, 0), 0)),
                  pl.BlockSpec(pool_w_bf.shape, lambda i: (0, 0, 0)),
                  pl.BlockSpec((1, pw_cols), lambda i: (0, 0)),
                  pl.BlockSpec(w_out_bf.shape, lambda i: (0, 0))],
        out_specs=pl.BlockSpec((tm, d), lambda i: (i, 0)),
        out_shape=jax.ShapeDtypeStruct((t, d), F32),
        scratch_shapes=[pltpu.VMEM((POOL_HALO + tm, pw_cols), F32)],
        compiler_params=_cparams(("arbitrary",)),
        name="mix_prompt",
    )(x, attn, u, u, pool_w_bf, pool_scale, w_out_bf)


def _mix_sample_kernel(x_ref, a_ref, u_ref, st_ref, pw_ref, ps_ref, wo_ref, o_ref,
                       *, dec, start_pos):
    def z_row(r):
        return st_ref[r] if r < POOL_PAD else u_ref[r - POOL_PAD]

    for i in range(dec):
        def window_sum(g, w, i=i):
            cs = slice(g * POOL_GW, (g + 1) * POOL_GW)
            acc = z_row(POOL_PAD + i)[:, cs]
            for k in range(1, w):
                acc = acc + z_row(POOL_PAD + i - k)[:, cs]
            return acc

        pool = _pool_groups(
            window_sum,
            lambda g, i=i: u_ref[i][:, g * POOL_GW:(g + 1) * POOL_GW],
            lambda w, i=i: float(min(start_pos + i + 1, w)),
            pw_ref, ps_ref)
        mixed = jnp.concatenate([a_ref[i], pool], axis=-1).astype(BF16)
        o_ref[i] = x_ref[i] + jnp.dot(mixed, wo_ref[...], preferred_element_type=F32)


def _mix_sample(x3, attn3, u3, state3, pool_w_bf, pool_scale, w_out_bf, start_pos):
    dec, nseq, d = x3.shape
    full = lambda a: pl.BlockSpec(a.shape, lambda i: (0,) * a.ndim)
    args = (x3, attn3, u3, state3, pool_w_bf, pool_scale, w_out_bf)
    return pl.pallas_call(
        functools.partial(_mix_sample_kernel, dec=dec, start_pos=start_pos),
        grid=(1,),
        in_specs=[full(a) for a in args],
        out_specs=full(x3),
        out_shape=jax.ShapeDtypeStruct((dec, nseq, d), F32),
        compiler_params=_cparams(("arbitrary",)),
        name="mix_sample",
    )(*args)


def _erf_gelu(x):
    return 0.5 * x * (1.0 + lax.erf(x * (2.0 ** -0.5)))


def _top_rows(x, n):
    rows = []
    for r in range(n):
        m = jnp.max(x, axis=0, keepdims=True)
        rows.append(m)
        if r + 1 < n:
            x = jnp.where(x == m, -jnp.inf, x)
    return rows


def _stack_rows(rows):
    n = len(rows)
    rid = lax.broadcasted_iota(jnp.int32, (n, LANES), 0)
    out = jnp.broadcast_to(rows[0], (n, LANES))
    for r in range(1, n):
        out = jnp.where(rid == r, rows[r], out)
    return out


def _peer_route_block(s1, s2):
    k = PEER_TOPK
    t1 = _top_rows(s1, k)
    t2 = _top_rows(s2, k)
    t1s = _stack_rows(t1)
    t2s = _stack_rows(t2)
    half = k // 2
    rid = lax.broadcasted_iota(jnp.int32, (half, LANES), 0)
    cands = [t1s[0:half] + t2[0], t1s[half:k] + t2[0], t1s[0:half] + t2[1]]
    for q in range(2, half):
        cands.append(jnp.where(rid < k // (q + 1), t1s[0:half] + t2[q], -jnp.inf))
    cands.append(t2s[half:k] + t1[0])
    cand = jnp.concatenate(cands, axis=0)
    tau = _top_rows(cand, k)[k - 1]
    top = t1[0] + t2[0]
    z = jnp.sum(jnp.where(cand >= tau, jnp.exp(cand - top), 0.0), axis=0, keepdims=True)
    thr = jnp.full(s1.shape, jnp.inf, F32)
    for q in range(k):
        thr = jnp.where(s1 + t2[q] >= tau, t2[q], thr)
    a = jnp.exp(s1 - t1[0])
    b = jnp.exp(s2 - t2[0]) / z
    return thr, a, b


def _peer_kernel(x_ref, g_ref, wq_ref, keys_ref, u_ref, vt_ref, o_ref,
                 xn_ref, s1_ref, s2_ref, thr_ref, a_ref, b_ref, pt_ref, acc_ref,
                 *, tt, ec):
    j = pl.program_id(1)
    nsb = tt // LANES
    c_per = ec // PEER_KEYS

    @pl.when(j == 0)
    def _route():
        xn = _rmsnorm_rows(x_ref[...], g_ref[...]).astype(BF16)
        xn_ref[...] = xn

        def head_scores(h, carry):
            qh = jnp.dot(xn_ref[...], wq_ref[h], preferred_element_type=F32)
            half = qh.shape[1] // 2
            for part, dst in ((0, s1_ref), (1, s2_ref)):
                qp = qh[:, part * half:(part + 1) * half].astype(BF16)
                st = lax.dot_general(keys_ref[h, part], qp, (((1,), (1,)), ((), ())),
                                     preferred_element_type=F32)
                for sb in range(nsb):
                    dst[h, sb] = st[:, sb * LANES:(sb + 1) * LANES]
            return carry

        lax.fori_loop(0, PEER_HEADS, head_scores, 0)

        def route(idx, carry):
            h, sb = idx // nsb, idx % nsb
            thr, a, b = _peer_route_block(s1_ref[h, sb], s2_ref[h, sb])
            thr_ref[h, sb] = thr
            a_ref[h, sb] = a
            b_ref[h, sb] = b
            return carry

        lax.fori_loop(0, PEER_HEADS * nsb, route, 0)
        acc_ref[...] = jnp.zeros(acc_ref.shape, F32)

    at = lax.dot_general(u_ref[...], xn_ref[...], (((1,), (1,)), ((), ())),
                         preferred_element_type=F32)

    def gates(cl, carry):
        c = j * c_per + cl
        r0 = pl.multiple_of(cl * PEER_KEYS, PEER_KEYS)
        for sb in range(nsb):
            w = jnp.zeros((PEER_KEYS, LANES), F32)
            for h in range(PEER_HEADS):
                sel = s2_ref[h, sb] >= thr_ref[h, sb, pl.ds(c, 1), :]
                w = w + jnp.where(sel, b_ref[h, sb] * a_ref[h, sb, pl.ds(c, 1), :], 0.0)
            pt_ref[pl.ds(r0, PEER_KEYS), sb * LANES:(sb + 1) * LANES] = w
        return carry

    lax.fori_loop(0, c_per, gates, 0)
    p = (pt_ref[...] * _erf_gelu(at)).astype(BF16)
    acc_ref[...] += jnp.dot(vt_ref[...], p, preferred_element_type=F32)

    @pl.when(j == pl.num_programs(1) - 1)
    def _finish():
        o_ref[...] = x_ref[...] + acc_ref[...].T


def _peer(x, g, wq_heads_bf, keys_bf, u_bf, vt_bf, tt, ec):
    t, d = x.shape
    n_exp = u_bf.shape[0]
    assert t % tt == 0 and n_exp % ec == 0 and tt % LANES == 0 and ec % PEER_KEYS == 0
    assert n_exp == PEER_KEYS * PEER_KEYS
    nsb = tt // LANES
    blk = lambda: pltpu.VMEM((PEER_HEADS, nsb, PEER_KEYS, LANES), F32)
    return pl.pallas_call(
        functools.partial(_peer_kernel, tt=tt, ec=ec),
        grid=(t // tt, n_exp // ec),
        in_specs=[pl.BlockSpec((tt, d), lambda i, j: (i, 0)),
                  pl.BlockSpec((1, d), lambda i, j: (0, 0)),
                  pl.BlockSpec(wq_heads_bf.shape, lambda i, j: (0, 0, 0)),
                  pl.BlockSpec(keys_bf.shape, lambda i, j: (0, 0, 0, 0)),
                  pl.BlockSpec((ec, d), lambda i, j: (j, 0)),
                  pl.BlockSpec((d, ec), lambda i, j: (0, j))],
        out_specs=pl.BlockSpec((tt, d), lambda i, j: (i, 0)),
        out_shape=jax.ShapeDtypeStruct((t, d), F32),
        scratch_shapes=[pltpu.VMEM((tt, d), BF16),
                        blk(), blk(), blk(), blk(), blk(),
                        pltpu.VMEM((ec, tt), F32),
                        pltpu.VMEM((d, tt), F32)],
        compiler_params=_cparams(("arbitrary", "arbitrary")),
        name="peer",
    )(x, g, wq_heads_bf, keys_bf, u_bf, vt_bf)


def _tail_kernel(x_ref, p_ref, gpl_ref, wg_ref, wpl_ref, gf_ref, o_ref):
    x = x_ref[...]
    xn = _rmsnorm_rows(x, gpl_ref[...]).astype(BF16)
    gate = jax.nn.sigmoid(jnp.dot(xn, wg_ref[...], preferred_element_type=F32))
    emb = jnp.dot(p_ref[...].astype(BF16), wpl_ref[...], preferred_element_type=F32)
    o_ref[...] = _rmsnorm_rows(x + gate * emb, gf_ref[...])


def _tail(x, p, g_pl, w_gate_bf, w_pl_bf, g_final, tm):
    t, d = x.shape
    pd = p.shape[1]
    return pl.pallas_call(
        _tail_kernel,
        grid=(t // tm,),
        in_specs=[pl.BlockSpec((tm, d), lambda i: (i, 0)),
                  pl.BlockSpec((tm, pd), lambda i: (i, 0)),
                  pl.BlockSpec((1, d), lambda i: (0, 0)),
                  pl.BlockSpec((d, d), lambda i: (0, 0)),
                  pl.BlockSpec((pd, d), lambda i: (0, 0)),
                  pl.BlockSpec((1, d), lambda i: (0, 0))],
        out_specs=pl.BlockSpec((tm, d), lambda i: (i, 0)),
        out_shape=jax.ShapeDtypeStruct((t, d), F32),
        compiler_params=_cparams(("arbitrary",)),
        name="tail",
    )(x, p, g_pl, w_gate_bf, w_pl_bf, g_final)


def _row_tile(t, want):
    tm = min(t, want)
    assert t % tm == 0
    return tm


def kernel(x_prompt, x_sample, cache_k, cache_v, page_table, state_pool, p_prompt, p_sample, norm_attn_g, w_in, lambda_q1, lambda_k1, lambda_q2, lambda_k2, subln_g, rel_bias, pool_w, pool_scale, w_out, norm_ffn_g, peer_wq, peer_keys, peer_u, peer_v, norm_pl_g, w_pl, w_pl_gate, final_norm_g):
    depth = w_in.shape[0]
    assert depth == 1
    batch, seq, d = x_prompt.shape
    nseq, dec, _ = x_sample.shape
    n_pages = page_table.shape[1]
    page = cache_k.shape[2]
    past_len = n_pages * page
    lam_init = 0.8 - 0.6 * math.exp(-0.3 * 0)
    i = 0

    row = lambda a: a.reshape(1, -1)
    w_in_bf = w_in[i].astype(BF16)
    pool_w_bf = pool_w[i].astype(BF16)
    w_out_bf = w_out[i].astype(BF16)
    qd = peer_wq.shape[2] // PEER_HEADS
    wq_heads_bf = peer_wq[i].reshape(d, PEER_HEADS, qd).transpose(1, 0, 2).astype(BF16)
    keys_bf = peer_keys[i].astype(BF16)
    u_bf = peer_u[i].astype(BF16)
    vt_bf = peer_v[i].T.astype(BF16)
    w_gate_bf = w_pl_gate[i].astype(BF16)
    w_pl_bf = w_pl[i].astype(BF16)
    rb_flat = rel_bias.reshape(-1)
    lam_rows = (row(lambda_q1[i]), row(lambda_k1[i]), row(lambda_q2[i]), row(lambda_k2[i]))
    sg = row(subln_g[i])

    tp = batch * seq
    xp = x_prompt.reshape(tp, d)
    q, k, v, u = _inproj(xp, row(norm_attn_g[i]), w_in_bf, _row_tile(tp, 512))
    attn = _attn_prompt(q, k, v, rb_flat, *lam_rows, sg, batch, seq, min(seq, 512), lam_init)
    x1 = _mix_prompt(xp, attn, u, pool_w_bf, row(pool_scale[i]), w_out_bf, seq, min(seq, 512))
    x2 = _peer(x1, row(norm_ffn_g[i]), wq_heads_bf, keys_bf, u_bf, vt_bf,
               _row_tile(tp, 512), min(u_bf.shape[0], 1024))
    y_prompt = _tail(x2, p_prompt[i].reshape(tp, -1), row(norm_pl_g[i]), w_gate_bf, w_pl_bf,
                     row(final_norm_g), _row_tile(tp, 512)).reshape(batch, seq, d)
    new_k_prompt = k.reshape(1, batch, seq, HEADS, HEAD_W)
    new_v_prompt = v.reshape(1, batch, seq, HEADS, HEAD_W)
    new_pool_prompt = u.reshape(batch, seq, -1)[:, seq - POOL_PAD:, :][None]

    ts = nseq * dec
    xs = x_sample.reshape(ts, d)
    qs, ks, vs, us = _inproj(xs, row(norm_attn_g[i]), w_in_bf, _row_tile(ts, 512))
    attn_s = _attn_sample(qs, ks, vs, cache_k, cache_v, i, page_table, rb_flat,
                          *lam_rows, sg, dec, lam_init, math.gcd(n_pages, 8))
    tok_major = lambda a: a.reshape(nseq, dec, -1).transpose(1, 0, 2)
    x1s = _mix_sample(tok_major(xs), tok_major(attn_s), tok_major(us),
                      state_pool[i].transpose(1, 0, 2), pool_w_bf, row(pool_scale[i]),
                      w_out_bf, past_len)
    x1s = x1s.transpose(1, 0, 2).reshape(ts, d)
    x2s = _peer(x1s, row(norm_ffn_g[i]), wq_heads_bf, keys_bf, u_bf, vt_bf,
                _row_tile(ts, 512), min(u_bf.shape[0], 1024))
    y_sample = _tail(x2s, p_sample[i].reshape(ts, -1), row(norm_pl_g[i]), w_gate_bf, w_pl_bf,
                     row(final_norm_g), _row_tile(ts, 512)).reshape(nseq, dec, d)
    new_k_sample = ks.reshape(1, nseq, dec, HEADS, HEAD_W)
    new_v_sample = vs.reshape(1, nseq, dec, HEADS, HEAD_W)
    new_pool_sample = jnp.concatenate(
        [state_pool[i][:, dec:, :], us.reshape(nseq, dec, -1)], axis=1)[None]

    return (y_prompt, y_sample, new_k_prompt, new_v_prompt, new_pool_prompt,
            new_k_sample, new_v_sample, new_pool_sample)
```

```python
import functools
import math

import jax
import jax.numpy as jnp
import numpy as np
from jax import lax
from jax.experimental import pallas as pl
from jax.experimental.pallas import tpu as pltpu

F32 = jnp.float32
BF16 = jnp.bfloat16

EPS = 1e-6
NEG_INF = -1e30
HEADS = 4
QK_DIM = 64
HEAD_W = 2 * QK_DIM
ATT_W = HEADS * HEAD_W
POOL_WINDOWS = (2, 4, 8, 16)
POOL_GW = 128
POOL_PAD = 15
POOL_HALO = 16
REL_BUCKETS = 32
REL_MAX_DIST = 128
PEER_HEADS = 8
PEER_KEYS = 128
PEER_TOPK = 16
LANES = 128
BF16_SUBLANES = 16
PEER_KSPLIT = 4
ONES_ROWS = 16
VMEM_LIMIT = 56 * 1024 * 1024


def _bucket_thresholds():
    exact = REL_BUCKETS // 2
    n = np.arange(0, 4 * REL_MAX_DIST)
    large = exact + (np.log(np.maximum(n, 1).astype(np.float32) / np.float32(exact))
                     / np.float32(math.log(REL_MAX_DIST / exact))
                     * np.float32(REL_BUCKETS - exact)).astype(np.int32)
    large = np.minimum(large, REL_BUCKETS - 1)
    bucket = np.where(n < exact, n, large)
    return [int(n[np.argmax(bucket >= b)]) for b in range(REL_BUCKETS)]


_BUCKET_START = _bucket_thresholds()
_FAR_DIST = _BUCKET_START[REL_BUCKETS - 1]


def _rel_bias_tile(rel, rb_ref, h):
    out = jnp.full(rel.shape, rb_ref[h], F32)
    for b in range(1, REL_BUCKETS):
        out = jnp.where(rel >= _BUCKET_START[b], rb_ref[b * HEADS + h], out)
    return out


def _rmsnorm_rows(x, g):
    return x * lax.rsqrt(jnp.mean(x * x, axis=-1, keepdims=True) + EPS) * g


def _cparams(sem):
    return pltpu.CompilerParams(dimension_semantics=sem, vmem_limit_bytes=VMEM_LIMIT)


def _inproj_kernel(x_ref, g_ref, w_ref, q_ref, k_ref, v_ref, u_ref, qb_ref, kb_ref, vtb_ref):
    xn = _rmsnorm_rows(x_ref[...], g_ref[...])
    proj = jnp.dot(xn.astype(BF16), w_ref[...], preferred_element_type=F32)
    q = proj[:, 0 * ATT_W:1 * ATT_W]
    k = proj[:, 1 * ATT_W:2 * ATT_W]
    v = proj[:, 2 * ATT_W:3 * ATT_W]
    q_ref[...] = q
    k_ref[...] = k
    v_ref[...] = v
    u_ref[...] = proj[:, 3 * ATT_W:4 * ATT_W]
    qb_ref[...] = (q * (QK_DIM ** -0.5)).astype(BF16)
    kb_ref[...] = k.astype(BF16)
    vtb_ref[...] = v.T.astype(BF16)


def _inproj(x, g, w_bf, tm):
    t, d = x.shape
    n_out = w_bf.shape[1]
    assert n_out == 4 * ATT_W and t % tm == 0
    out = jax.ShapeDtypeStruct((t, ATT_W), F32)
    out_bf = jax.ShapeDtypeStruct((t, ATT_W), BF16)
    row = pl.BlockSpec((tm, ATT_W), lambda i: (i, 0))
    return pl.pallas_call(
        _inproj_kernel,
        grid=(t // tm,),
        in_specs=[pl.BlockSpec((tm, d), lambda i: (i, 0)),
                  pl.BlockSpec((1, d), lambda i: (0, 0)),
                  pl.BlockSpec((d, n_out), lambda i: (0, 0))],
        out_specs=[row] * 6 + [pl.BlockSpec((ATT_W, tm), lambda i: (0, i))],
        out_shape=[out] * 4 + [out_bf] * 2 + [jax.ShapeDtypeStruct((ATT_W, t), BF16)],
        compiler_params=_cparams(("arbitrary",)),
        name="inproj",
    )(x, g, w_bf)


def _lambda_full(lq1, lk1, lq2, lk2, lam_init):
    return (jnp.exp(jnp.sum(lq1 * lk1, axis=-1, keepdims=True))
            - jnp.exp(jnp.sum(lq2 * lk2, axis=-1, keepdims=True)) + lam_init)


def _head_out(acc1, l1, acc2, l2, lam, subln_g, lam_init):
    o = acc1 / l1 - lam * (acc2 / l2)
    return _rmsnorm_rows(o, subln_g) * (1.0 - lam_init)


def _attn_prompt_kernel(rb_ref, q_ref, k_ref, vt_ref, lq1_ref, lk1_ref, lq2_ref, lk2_ref, sg_ref,
                        o_ref, bias_ref, ve_ref, m_ref, acc_ref, *, tb, lam_init):
    b, qi, kj = pl.program_id(0), pl.program_id(1), pl.program_id(2)

    @pl.when((b == 0) & (qi == 0) & (kj == 0))
    def _build_constants():
        key = lax.broadcasted_iota(jnp.int32, (tb, tb), 0)
        qry = lax.broadcasted_iota(jnp.int32, (tb, tb), 1)
        rel = qry - key
        for h in range(HEADS):
            diag = _rel_bias_tile(rel, rb_ref, h)
            bias_ref[0, h] = jnp.where(rel >= 0, diag, NEG_INF)
            bias_ref[1, h] = _rel_bias_tile(rel + tb, rb_ref, h)
            bias_ref[2, h] = jnp.full((tb, tb), rb_ref[(REL_BUCKETS - 1) * HEADS + h], F32)
            ve_ref[h, HEAD_W:HEAD_W + ONES_ROWS, :] = jnp.ones((ONES_ROWS, tb), ve_ref.dtype)

    @pl.when(kj == 0)
    def _init():
        m_ref[...] = jnp.full(m_ref.shape, -jnp.inf, F32)
        acc_ref[...] = jnp.zeros(acc_ref.shape, F32)

    @pl.when(kj <= qi)
    def _step():
        sel = jnp.where(kj == qi, 0, jnp.where(kj == qi - 1, 1, 2))
        for h in range(HEADS):
            ve_ref[h, 0:HEAD_W, :] = vt_ref[h * HEAD_W:(h + 1) * HEAD_W, :]
        for h in range(HEADS):
            for mp in range(2):
                c0 = h * HEAD_W + mp * QK_DIM
                idx = h * 2 + mp
                st = lax.dot_general(k_ref[:, c0:c0 + QK_DIM], q_ref[:, c0:c0 + QK_DIM],
                                     (((1,), (1,)), ((), ())), preferred_element_type=F32)
                st = st + bias_ref[sel, h]
                m_old = m_ref[idx]
                m_new = jnp.maximum(m_old, jnp.max(st, axis=0, keepdims=True))
                alpha = jnp.exp(m_old - m_new)
                p = jnp.exp(st - m_new).astype(BF16)
                acc_ref[idx] = alpha * acc_ref[idx] + jnp.dot(
                    ve_ref[h], p, preferred_element_type=F32)
                m_ref[idx] = m_new

    @pl.when(kj == qi)
    def _finish():
        lam = _lambda_full(lq1_ref[...], lk1_ref[...], lq2_ref[...], lk2_ref[...], lam_init)
        for h in range(HEADS):
            a1, a2 = acc_ref[2 * h], acc_ref[2 * h + 1]
            o = (a1[0:HEAD_W] / a1[HEAD_W:HEAD_W + 1]
                 - lam * (a2[0:HEAD_W] / a2[HEAD_W:HEAD_W + 1]))
            ms = jnp.mean(o * o, axis=0, keepdims=True)
            on = o * lax.rsqrt(ms + EPS) * sg_ref[...] * (1.0 - lam_init)
            o_ref[:, h * HEAD_W:(h + 1) * HEAD_W] = on.T


def _attn_prompt(q, k, vt, rb_flat, lq1, lk1, lq2, lk2, subln_col, batch, seq, tb, lam_init):
    t = q.shape[0]
    assert t == batch * seq and seq % tb == 0 and tb >= _FAR_DIST and q.dtype == BF16
    nq = seq // tb
    small = lambda n: pl.BlockSpec((1, n), lambda b, i, j: (0, 0))
    k_spec = pl.BlockSpec((tb, ATT_W), lambda b, i, j: (b * nq + jnp.minimum(j, i), 0))
    vt_spec = pl.BlockSpec((ATT_W, tb), lambda b, i, j: (0, b * nq + jnp.minimum(j, i)))
    q_spec = pl.BlockSpec((tb, ATT_W), lambda b, i, j: (b * nq + i, 0))
    return pl.pallas_call(
        functools.partial(_attn_prompt_kernel, tb=tb, lam_init=lam_init),
        grid=(batch, nq, nq),
        in_specs=[pl.BlockSpec(memory_space=pltpu.SMEM),
                  q_spec, k_spec, vt_spec,
                  small(QK_DIM), small(QK_DIM), small(QK_DIM), small(QK_DIM),
                  pl.BlockSpec((HEAD_W, 1), lambda b, i, j: (0, 0))],
        out_specs=q_spec,
        out_shape=jax.ShapeDtypeStruct((t, ATT_W), F32),
        scratch_shapes=[pltpu.VMEM((3, HEADS, tb, tb), F32),
                        pltpu.VMEM((HEADS, HEAD_W + ONES_ROWS, tb), BF16),
                        pltpu.VMEM((2 * HEADS, 1, tb), F32),
                        pltpu.VMEM((2 * HEADS, HEAD_W + ONES_ROWS, tb), F32)],
        compiler_params=_cparams(("arbitrary", "arbitrary", "arbitrary")),
        name="attn_prompt",
    )(rb_flat, q, k, vt, lq1, lk1, lq2, lk2, subln_col)


QROWS = 8


def _attn_sample_kernel(pt_ref, rb_ref, q_ref, kn_ref, vn_ref, *rest,
                        n_steps, pp, page, dec, lam_init):
    del pt_ref
    k_refs, v_refs = rest[:pp], rest[pp:2 * pp]
    (lq1_ref, lk1_ref, lq2_ref, lk2_ref, sg_ref, o_ref,
     q8_ref, qcat_ref, bias_ref, m_ref, l_ref, acc_ref) = rest[2 * pp:]
    b, p = pl.program_id(0), pl.program_id(1)
    hrows = 2 * QROWS
    rows = HEADS * hrows
    prow = page * HEADS
    width = pp * prow
    past_len = n_steps * pp * page
    row_i = lax.broadcasted_iota(jnp.int32, (rows, 1), 0) % QROWS

    def per_head(fn):
        return jnp.concatenate([fn(h) for h in range(HEADS)], axis=0)

    @pl.when(b == 0)
    def _build_bias():
        col = lax.broadcasted_iota(jnp.int32, (hrows, width), 1)
        k_pos = p * (pp * page) + col // HEADS
        rel = past_len + lax.broadcasted_iota(jnp.int32, (hrows, width), 0) % QROWS - k_pos
        bias_ref[p] = per_head(lambda h: jnp.where(col % HEADS == h,
                                                   _rel_bias_tile(rel, rb_ref, h), NEG_INF))

    @pl.when(p == 0)
    def _init():
        q8_ref[...] = jnp.zeros(q8_ref.shape, F32)
        q8_ref[0:dec, :] = q_ref[0] * (QK_DIM ** -0.5)
        col = lax.broadcasted_iota(jnp.int32, (QROWS, HEAD_W), 1)
        for h in range(HEADS):
            qh = q8_ref[:, h * HEAD_W:(h + 1) * HEAD_W]
            for mp in range(2):
                r0 = h * hrows + mp * QROWS
                qcat_ref[r0:r0 + QROWS, :] = jnp.where(col // QK_DIM == mp, qh, 0.0)
        m_ref[...] = jnp.full(m_ref.shape, -jnp.inf, F32)
        l_ref[...] = jnp.zeros(l_ref.shape, F32)
        acc_ref[...] = jnp.zeros(acc_ref.shape, F32)

    qcat = qcat_ref[...]
    qcat_bf = qcat.astype(BF16)
    s = jnp.concatenate(
        [lax.dot_general(qcat_bf, k_refs[r][...].astype(BF16), (((1,), (1,)), ((), ())),
                         preferred_element_type=F32) for r in range(pp)], axis=1)
    s = s + bias_ref[p]
    m_old = m_ref[...]
    m_new = jnp.maximum(m_old, jnp.max(s, axis=-1, keepdims=True))
    alpha = jnp.exp(m_old - m_new)
    pr = jnp.exp(s - m_new)
    l_ref[...] = alpha * l_ref[...] + jnp.sum(pr, axis=-1, keepdims=True)
    pv = jnp.dot(pr[:, 0:prow].astype(BF16), v_refs[0][...].astype(BF16),
                 preferred_element_type=F32)
    for r in range(1, pp):
        pv = pv + jnp.dot(pr[:, r * prow:(r + 1) * prow].astype(BF16),
                          v_refs[r][...].astype(BF16), preferred_element_type=F32)
    acc_ref[...] = alpha * acc_ref[...] + pv
    m_ref[...] = m_new

    @pl.when(p == n_steps - 1)
    def _finish():
        def head_row(ref, j, h):
            return ref[0, j:j + 1, h * HEAD_W:(h + 1) * HEAD_W]

        s_new = []
        for j in range(dec):
            sj = per_head(lambda h, j=j: jnp.sum(
                qcat[h * hrows:(h + 1) * hrows] * head_row(kn_ref, j, h), axis=-1, keepdims=True)
                + _rel_bias_tile(row_i[0:hrows] - j, rb_ref, h))
            s_new.append(jnp.where(row_i >= j, sj, NEG_INF))
        m_old2 = m_ref[...]
        m_fin = m_old2
        for sj in s_new:
            m_fin = jnp.maximum(m_fin, sj)
        alpha2 = jnp.exp(m_old2 - m_fin)
        l_fin = alpha2 * l_ref[...]
        acc_fin = alpha2 * acc_ref[...]
        for j, sj in enumerate(s_new):
            pj = jnp.exp(sj - m_fin)
            l_fin = l_fin + pj
            acc_fin = acc_fin + pj * per_head(
                lambda h, j=j: jnp.broadcast_to(head_row(vn_ref, j, h), (hrows, HEAD_W)))
        lam = _lambda_full(lq1_ref[...], lk1_ref[...], lq2_ref[...], lk2_ref[...], lam_init)
        for h in range(HEADS):
            r1, r2 = h * hrows, h * hrows + QROWS
            oh = _head_out(acc_fin[r1:r1 + QROWS], l_fin[r1:r1 + QROWS],
                           acc_fin[r2:r2 + QROWS], l_fin[r2:r2 + QROWS],
                           lam, sg_ref[...], lam_init)
            o_ref[0, :, h * HEAD_W:(h + 1) * HEAD_W] = oh[0:dec]


def _attn_sample(q, k_new, v_new, cache_k, cache_v, layer, page_table, rb_flat,
                 lq1, lk1, lq2, lk2, subln_g, dec, lam_init, pp):
    nseq, n_pages = page_table.shape
    depth, n_phys, page = cache_k.shape[:3]
    assert dec <= QROWS and page >= _FAR_DIST and n_pages % pp == 0
    n_steps = n_pages // pp
    q3 = q.reshape(nseq, dec, ATT_W)
    kn3 = k_new.reshape(nseq, dec, ATT_W)
    vn3 = v_new.reshape(nseq, dec, ATT_W)
    prow = page * HEADS
    kc = cache_k.reshape(depth * n_phys * prow, HEAD_W)
    vc = cache_v.reshape(depth * n_phys * prow, HEAD_W)
    rows = 2 * HEADS * QROWS
    seq_spec = pl.BlockSpec((1, dec, ATT_W), lambda b, p, pt: (b, 0, 0))

    def page_spec(r):
        return pl.BlockSpec((prow, HEAD_W),
                            lambda b, p, pt: (layer * n_phys + pt[b, p * pp + r], 0))

    pages = [page_spec(r) for r in range(pp)]
    small = lambda n: pl.BlockSpec((1, n), lambda b, p, pt: (0, 0))
    grid_spec = pltpu.PrefetchScalarGridSpec(
        num_scalar_prefetch=1,
        grid=(nseq, n_steps),
        in_specs=[pl.BlockSpec(memory_space=pltpu.SMEM), seq_spec, seq_spec, seq_spec]
        + pages + pages
        + [small(QK_DIM), small(QK_DIM), small(QK_DIM), small(QK_DIM), small(HEAD_W)],
        out_specs=seq_spec,
        scratch_shapes=[pltpu.VMEM((QROWS, ATT_W), F32),
                        pltpu.VMEM((rows, HEAD_W), F32),
                        pltpu.VMEM((n_steps, rows, pp * prow), F32),
                        pltpu.VMEM((rows, 1), F32),
                        pltpu.VMEM((rows, 1), F32),
                        pltpu.VMEM((rows, HEAD_W), F32)])
    out = pl.pallas_call(
        functools.partial(_attn_sample_kernel, n_steps=n_steps, pp=pp, page=page, dec=dec,
                          lam_init=lam_init),
        grid_spec=grid_spec,
        out_shape=jax.ShapeDtypeStruct((nseq, dec, ATT_W), F32),
        compiler_params=_cparams(("arbitrary", "arbitrary")),
        name="attn_sample",
    )(page_table, rb_flat, q3, kn3, vn3, *([kc] * pp), *([vc] * pp), lq1, lk1, lq2, lk2, subln_g)
    return out.reshape(nseq * dec, ATT_W)


def _pool_groups(window_sum, u_of, cnt_of, pw_ref, ps_ref):
    outs = []
    for g, w in enumerate(POOL_WINDOWS):
        d = window_sum(g, w) / cnt_of(w) - u_of(g)
        outs.append(jnp.dot(d.astype(BF16), pw_ref[g], preferred_element_type=F32))
    return jnp.concatenate(outs, axis=-1) * ps_ref[...]


def _mix_prompt_kernel(x_ref, a_ref, u_ref, halo_ref, pw_ref, ps_ref, wo_ref, o_ref, z_ref,
                       *, tm, tiles_per_seq):
    i = pl.program_id(0)
    t_in_seq = i % tiles_per_seq
    z_ref[0:POOL_HALO, :] = jnp.where(t_in_seq == 0, 0.0, halo_ref[...])
    z_ref[POOL_HALO:POOL_HALO + tm, :] = u_ref[...]
    pos = t_in_seq * tm + lax.broadcasted_iota(jnp.int32, (tm, 1), 0)

    def window_sum(g, w):
        cs = slice(g * POOL_GW, (g + 1) * POOL_GW)
        acc = z_ref[POOL_HALO:POOL_HALO + tm, cs]
        for k in range(1, w):
            acc = acc + z_ref[POOL_HALO - k:POOL_HALO - k + tm, cs]
        return acc

    pool = _pool_groups(
        window_sum,
        lambda g: u_ref[:, g * POOL_GW:(g + 1) * POOL_GW],
        lambda w: jnp.minimum(pos + 1, w).astype(F32),
        pw_ref, ps_ref)
    mixed = jnp.concatenate([a_ref[...], pool], axis=-1).astype(BF16)
    o_ref[...] = x_ref[...] + jnp.dot(mixed, wo_ref[...], preferred_element_type=F32)


def _mix_prompt(x, attn, u, pool_w_bf, pool_scale, w_out_bf, seq, tm):
    t, d = x.shape
    pw_cols = u.shape[1]
    assert seq % tm == 0 and tm % POOL_HALO == 0
    halo_blocks = tm // POOL_HALO
    return pl.pallas_call(
        functools.partial(_mix_prompt_kernel, tm=tm, tiles_per_seq=seq // tm),
        grid=(t // tm,),
        in_specs=[pl.BlockSpec((tm, d), lambda i: (i, 0)),
                  pl.BlockSpec((tm, ATT_W), lambda i: (i, 0)),
                  pl.BlockSpec((tm, pw_cols), lambda i: (i, 0)),
                  pl.BlockSpec((POOL_HALO, pw_cols),
                               lambda i: (jnp.maximum(i * halo_blocks - 1, 0), 0)),
                  pl.BlockSpec(pool_w_bf.shape, lambda i: (0, 0, 0)),
                  pl.BlockSpec((1, pw_cols), lambda i: (0, 0)),
                  pl.BlockSpec(w_out_bf.shape, lambda i: (0, 0))],
        out_specs=pl.BlockSpec((tm, d), lambda i: (i, 0)),
        out_shape=jax.ShapeDtypeStruct((t, d), F32),
        scratch_shapes=[pltpu.VMEM((POOL_HALO + tm, pw_cols), F32)],
        compiler_params=_cparams(("arbitrary",)),
        name="mix_prompt",
    )(x, attn, u, u, pool_w_bf, pool_scale, w_out_bf)


def _mix_sample_kernel(x_ref, a_ref, u_ref, st_ref, pw_ref, ps_ref, wo_ref, o_ref,
                       *, dec, start_pos):
    def z_row(r):
        return st_ref[r] if r < POOL_PAD else u_ref[r - POOL_PAD]

    for i in range(dec):
        def window_sum(g, w, i=i):
            cs = slice(g * POOL_GW, (g + 1) * POOL_GW)
            acc = z_row(POOL_PAD + i)[:, cs]
            for k in range(1, w):
                acc = acc + z_row(POOL_PAD + i - k)[:, cs]
            return acc

        pool = _pool_groups(
            window_sum,
            lambda g, i=i: u_ref[i][:, g * POOL_GW:(g + 1) * POOL_GW],
            lambda w, i=i: float(min(start_pos + i + 1, w)),
            pw_ref, ps_ref)
        mixed = jnp.concatenate([a_ref[i], pool], axis=-1).astype(BF16)
        o_ref[i] = x_ref[i] + jnp.dot(mixed, wo_ref[...], preferred_element_type=F32)


def _mix_sample(x3, attn3, u3, state3, pool_w_bf, pool_scale, w_out_bf, start_pos):
    dec, nseq, d = x3.shape
    full = lambda a: pl.BlockSpec(a.shape, lambda i: (0,) * a.ndim)
    args = (x3, attn3, u3, state3, pool_w_bf, pool_scale, w_out_bf)
    return pl.pallas_call(
        functools.partial(_mix_sample_kernel, dec=dec, start_pos=start_pos),
        grid=(1,),
        in_specs=[full(a) for a in args],
        out_specs=full(x3),
        out_shape=jax.ShapeDtypeStruct((dec, nseq, d), F32),
        compiler_params=_cparams(("arbitrary",)),
        name="mix_sample",
    )(*args)


def _erf_gelu(x):
    return 0.5 * x * (1.0 + lax.erf(x * (2.0 ** -0.5)))


def _top_rows(x, n, with_rank=False):
    rows = []
    rank = jnp.full(x.shape, float(n), F32) if with_rank else None
    for r in range(n):
        m = jnp.max(x, axis=0, keepdims=True)
        rows.append(m)
        hit = x == m
        if with_rank:
            rank = jnp.where(hit, float(r), rank)
        if r + 1 < n:
            x = jnp.where(hit, -jnp.inf, x)
    return (rows, rank) if with_rank else rows


def _stack_rows(rows):
    n = len(rows)
    rid = lax.broadcasted_iota(jnp.int32, (n, LANES), 0)
    out = jnp.broadcast_to(rows[0], (n, LANES))
    for r in range(1, n):
        out = jnp.where(rid == r, rows[r], out)
    return out


def _peer_route_block(s1, s2):
    k = PEER_TOPK
    t1 = _top_rows(s1, k)
    t2, rank2 = _top_rows(s2, k, with_rank=True)
    t1s = _stack_rows(t1)
    t2s = _stack_rows(t2)
    half = k // 2
    rid = lax.broadcasted_iota(jnp.int32, (half, LANES), 0)
    cands = [t1s[0:half] + t2[0], t1s[half:k] + t2[0], t1s[0:half] + t2[1]]
    for q in range(2, half):
        cands.append(jnp.where(rid < k // (q + 1), t1s[0:half] + t2[q], -jnp.inf))
    cands.append(t2s[half:k] + t1[0])
    cand = jnp.concatenate(cands, axis=0)
    tau = _top_rows(cand, k)[k - 1]
    top = t1[0] + t2[0]
    z = jnp.sum(jnp.where(cand >= tau, jnp.exp(cand - top), 0.0), axis=0, keepdims=True)
    cnt = jnp.zeros(s1.shape, F32)
    for q in range(k):
        cnt = jnp.where(s1 + t2[q] >= tau, float(q + 1), cnt)
    a = jnp.exp(s1 - t1[0])
    b = jnp.exp(s2 - t2[0]) / z
    return cnt, rank2, a, b


def _peer_kernel(x_ref, g_ref, wq_ref, keys_ref, u_ref, vt_ref, o_ref,
                 xn_ref, s1_ref, s2_ref, cnt_ref, a_ref, rank_ref, b_ref, w0_ref, w1_ref, acc_ref,
                 *, tt, ec):
    j = pl.program_id(1)
    n_chunks = pl.num_programs(1)
    nsb = tt // LANES
    c_per = ec // PEER_KEYS
    pk = BF16_SUBLANES

    def gates(chunk, w_ref, cls=range(c_per)):
        zero = jnp.zeros((pk, LANES), BF16)
        for cl in cls:
            c = chunk * c_per + cl
            for sb in range(nsb):
                w = [zero] * (PEER_KEYS // pk)
                for h in range(PEER_HEADS):
                    cnt = jnp.broadcast_to(cnt_ref[h, sb, pl.ds(c, 1), :], (pk, LANES)).astype(BF16)
                    a = jnp.broadcast_to(a_ref[h, sb, pl.ds(c, 1), :], (pk, LANES)).astype(BF16)
                    for g in range(PEER_KEYS // pk):
                        rows = slice(g * pk, (g + 1) * pk)
                        sel = rank_ref[h, sb, rows, :] < cnt
                        w[g] = w[g] + jnp.where(sel, b_ref[h, sb, rows, :] * a, zero)
                for g in range(PEER_KEYS // pk):
                    w_ref[cl * PEER_KEYS + g * pk:cl * PEER_KEYS + (g + 1) * pk,
                          sb * LANES:(sb + 1) * LANES] = w[g]

    @pl.when(j == 0)
    def _route():
        xn = _rmsnorm_rows(x_ref[...], g_ref[...]).astype(BF16)
        xn_ref[...] = xn

        def head_scores(h, carry):
            qh = jnp.dot(xn_ref[...], wq_ref[h], preferred_element_type=F32)
            half = qh.shape[1] // 2
            for part, dst in ((0, s1_ref), (1, s2_ref)):
                qp = qh[:, part * half:(part + 1) * half].astype(BF16)
                st = lax.dot_general(keys_ref[h, part], qp, (((1,), (1,)), ((), ())),
                                     preferred_element_type=F32)
                for sb in range(nsb):
                    dst[h, sb] = st[:, sb * LANES:(sb + 1) * LANES]
            return carry

        lax.fori_loop(0, PEER_HEADS, head_scores, 0)

        def route(idx, carry):
            h, sb = idx // nsb, idx % nsb
            cnt, rank2, a, b = _peer_route_block(s1_ref[h, sb], s2_ref[h, sb])
            cnt_ref[h, sb] = cnt
            a_ref[h, sb] = a
            rank_ref[h, sb] = rank2.astype(BF16)
            b_ref[h, sb] = b.astype(BF16)
            return carry

        lax.fori_loop(0, PEER_HEADS * nsb, route, 0)
        acc_ref[...] = jnp.zeros(acc_ref.shape, F32)

        gates(0, w0_ref)

    def step(w_cur_ref, w_next_ref):
        nxt = (j + 1) % n_chunks
        d = xn_ref.shape[1]
        kd, ke, cg = d // PEER_KSPLIT, ec // PEER_KSPLIT, c_per // (2 * PEER_KSPLIT)
        at = None
        for kt in range(PEER_KSPLIT):
            part = lax.dot_general(u_ref[:, kt * kd:(kt + 1) * kd], xn_ref[:, kt * kd:(kt + 1) * kd],
                                   (((1,), (1,)), ((), ())), preferred_element_type=F32)
            at = part if at is None else at + part
            gates(nxt, w_next_ref, range(kt * cg, (kt + 1) * cg))
        out = None
        for kt in range(PEER_KSPLIT):
            rows = slice(kt * ke, (kt + 1) * ke)
            p = w_cur_ref[rows, :] * _erf_gelu(at[rows]).astype(BF16)
            part = jnp.dot(vt_ref[:, rows], p, preferred_element_type=F32)
            out = part if out is None else out + part
            gates(nxt, w_next_ref, range((PEER_KSPLIT + kt) * cg, (PEER_KSPLIT + kt + 1) * cg))
        acc_ref[...] += out

    @pl.when(j % 2 == 0)
    def _even():
        step(w0_ref, w1_ref)

    @pl.when(j % 2 == 1)
    def _odd():
        step(w1_ref, w0_ref)

    @pl.when(j == n_chunks - 1)
    def _finish():
        o_ref[...] = x_ref[...] + acc_ref[...].T


def _peer(x, g, wq_heads_bf, keys_bf, u_bf, vt_bf, tt, ec):
    t, d = x.shape
    n_exp = u_bf.shape[0]
    assert t % tt == 0 and n_exp % ec == 0 and tt % LANES == 0 and ec % PEER_KEYS == 0
    assert n_exp == PEER_KEYS * PEER_KEYS
    nsb = tt // LANES
    assert (ec // PEER_KEYS) % (2 * PEER_KSPLIT) == 0 and d % PEER_KSPLIT == 0
    blk = lambda dt: pltpu.VMEM((PEER_HEADS, nsb, PEER_KEYS, LANES), dt)
    return pl.pallas_call(
        functools.partial(_peer_kernel, tt=tt, ec=ec),
        grid=(t // tt, n_exp // ec),
        in_specs=[pl.BlockSpec((tt, d), lambda i, j: (i, 0)),
                  pl.BlockSpec((1, d), lambda i, j: (0, 0)),
                  pl.BlockSpec(wq_heads_bf.shape, lambda i, j: (0, 0, 0)),
                  pl.BlockSpec(keys_bf.shape, lambda i, j: (0, 0, 0, 0)),
                  pl.BlockSpec((ec, d), lambda i, j: (j, 0)),
                  pl.BlockSpec((d, ec), lambda i, j: (0, j))],
        out_specs=pl.BlockSpec((tt, d), lambda i, j: (i, 0)),
        out_shape=jax.ShapeDtypeStruct((t, d), F32),
        scratch_shapes=[pltpu.VMEM((tt, d), BF16),
                        blk(F32), blk(F32), blk(F32), blk(F32), blk(BF16), blk(BF16),
                        pltpu.VMEM((ec, tt), BF16),
                        pltpu.VMEM((ec, tt), BF16),
                        pltpu.VMEM((d, tt), F32)],
        compiler_params=_cparams(("arbitrary", "arbitrary")),
        name="peer",
    )(x, g, wq_heads_bf, keys_bf, u_bf, vt_bf)


def _tail_kernel(x_ref, p_ref, gpl_ref, wg_ref, wpl_ref, gf_ref, o_ref):
    x = x_ref[...]
    xn = _rmsnorm_rows(x, gpl_ref[...]).astype(BF16)
    gate = jax.nn.sigmoid(jnp.dot(xn, wg_ref[...], preferred_element_type=F32))
    emb = jnp.dot(p_ref[...].astype(BF16), wpl_ref[...], preferred_element_type=F32)
    o_ref[...] = _rmsnorm_rows(x + gate * emb, gf_ref[...])


def _tail(x, p, g_pl, w_gate_bf, w_pl_bf, g_final, tm):
    t, d = x.shape
    pd = p.shape[1]
    return pl.pallas_call(
        _tail_kernel,
        grid=(t // tm,),
        in_specs=[pl.BlockSpec((tm, d), lambda i: (i, 0)),
                  pl.BlockSpec((tm, pd), lambda i: (i, 0)),
                  pl.BlockSpec((1, d), lambda i: (0, 0)),
                  pl.BlockSpec((d, d), lambda i: (0, 0)),
                  pl.BlockSpec((pd, d), lambda i: (0, 0)),
                  pl.BlockSpec((1, d), lambda i: (0, 0))],
        out_specs=pl.BlockSpec((tm, d), lambda i: (i, 0)),
        out_shape=jax.ShapeDtypeStruct((t, d), F32),
        compiler_params=_cparams(("arbitrary",)),
        name="tail",
    )(x, p, g_pl, w_gate_bf, w_pl_bf, g_final)


def _row_tile(t, want):
    tm = min(t, want)
    assert t % tm == 0
    return tm


def kernel(x_prompt, x_sample, cache_k, cache_v, page_table, state_pool, p_prompt, p_sample, norm_attn_g, w_in, lambda_q1, lambda_k1, lambda_q2, lambda_k2, subln_g, rel_bias, pool_w, pool_scale, w_out, norm_ffn_g, peer_wq, peer_keys, peer_u, peer_v, norm_pl_g, w_pl, w_pl_gate, final_norm_g):
    depth = w_in.shape[0]
    assert depth == 1
    batch, seq, d = x_prompt.shape
    nseq, dec, _ = x_sample.shape
    n_pages = page_table.shape[1]
    page = cache_k.shape[2]
    past_len = n_pages * page
    lam_init = 0.8 - 0.6 * math.exp(-0.3 * 0)
    i = 0

    row = lambda a: a.reshape(1, -1)
    w_in_bf = w_in[i].astype(BF16)
    pool_w_bf = pool_w[i].astype(BF16)
    w_out_bf = w_out[i].astype(BF16)
    qd = peer_wq.shape[2] // PEER_HEADS
    wq_heads_bf = peer_wq[i].reshape(d, PEER_HEADS, qd).transpose(1, 0, 2).astype(BF16)
    keys_bf = peer_keys[i].astype(BF16)
    u_bf = peer_u[i].astype(BF16)
    vt_bf = peer_v[i].T.astype(BF16)
    w_gate_bf = w_pl_gate[i].astype(BF16)
    w_pl_bf = w_pl[i].astype(BF16)
    rb_flat = rel_bias.reshape(-1)
    lam_rows = (row(lambda_q1[i]), row(lambda_k1[i]), row(lambda_q2[i]), row(lambda_k2[i]))
    sg = row(subln_g[i])

    tp = batch * seq
    xp = x_prompt.reshape(tp, d)
    _, k, v, u, qb, kb, vtb = _inproj(xp, row(norm_attn_g[i]), w_in_bf, _row_tile(tp, 512))
    attn = _attn_prompt(qb, kb, vtb, rb_flat, *lam_rows, subln_g[i].reshape(-1, 1),
                        batch, seq, min(seq, 512), lam_init)
    x1 = _mix_prompt(xp, attn, u, pool_w_bf, row(pool_scale[i]), w_out_bf, seq, min(seq, 512))
    x2 = _peer(x1, row(norm_ffn_g[i]), wq_heads_bf, keys_bf, u_bf, vt_bf,
               _row_tile(tp, 512), min(u_bf.shape[0], 1024))
    y_prompt = _tail(x2, p_prompt[i].reshape(tp, -1), row(norm_pl_g[i]), w_gate_bf, w_pl_bf,
                     row(final_norm_g), _row_tile(tp, 512)).reshape(batch, seq, d)
    new_k_prompt = k.reshape(1, batch, seq, HEADS, HEAD_W)
    new_v_prompt = v.reshape(1, batch, seq, HEADS, HEAD_W)
    new_pool_prompt = u.reshape(batch, seq, -1)[:, seq - POOL_PAD:, :][None]

    ts = nseq * dec
    xs = x_sample.reshape(ts, d)
    qs, ks, vs, us, _, _, _ = _inproj(xs, row(norm_attn_g[i]), w_in_bf, _row_tile(ts, 512))
    attn_s = _attn_sample(qs, ks, vs, cache_k, cache_v, i, page_table, rb_flat,
                          *lam_rows, sg, dec, lam_init, math.gcd(n_pages, 8))
    tok_major = lambda a: a.reshape(nseq, dec, -1).transpose(1, 0, 2)
    x1s = _mix_sample(tok_major(xs), tok_major(attn_s), tok_major(us),
                      state_pool[i].transpose(1, 0, 2), pool_w_bf, row(pool_scale[i]),
                      w_out_bf, past_len)
    x1s = x1s.transpose(1, 0, 2).reshape(ts, d)
    x2s = _peer(x1s, row(norm_ffn_g[i]), wq_heads_bf, keys_bf, u_bf, vt_bf,
                _row_tile(ts, 512), min(u_bf.shape[0], 1024))
    y_sample = _tail(x2s, p_sample[i].reshape(ts, -1), row(norm_pl_g[i]), w_gate_bf, w_pl_bf,
                     row(final_norm_g), _row_tile(ts, 512)).reshape(nseq, dec, d)
    new_k_sample = ks.reshape(1, nseq, dec, HEADS, HEAD_W)
    new_v_sample = vs.reshape(1, nseq, dec, HEADS, HEAD_W)
    new_pool_sample = jnp.concatenate(
        [state_pool[i][:, dec:, :], us.reshape(nseq, dec, -1)], axis=1)[None]

    return (y_prompt, y_sample, new_k_prompt, new_v_prompt, new_pool_prompt,
            new_k_sample, new_v_sample, new_pool_sample)
```

```python
import functools
import math

import jax
import jax.numpy as jnp
import numpy as np
from jax import lax
from jax.experimental import pallas as pl
from jax.experimental.pallas import tpu as pltpu

F32 = jnp.float32
BF16 = jnp.bfloat16

EPS = 1e-6
NEG_INF = -1e30
HEADS = 4
QK_DIM = 64
HEAD_W = 2 * QK_DIM
ATT_W = HEADS * HEAD_W
POOL_WINDOWS = (2, 4, 8, 16)
POOL_GW = 128
POOL_PAD = 15
POOL_HALO = 16
REL_BUCKETS = 32
REL_MAX_DIST = 128
PEER_HEADS = 8
PEER_KEYS = 128
PEER_TOPK = 16
LANES = 128
BF16_SUBLANES = 16
PEER_KSPLIT = 4
GATE_ROWS = 32
A_OFF = PEER_KEYS + 8
B_OFF = PEER_KEYS + BF16_SUBLANES
ONES_ROWS = 16
VMEM_LIMIT = 56 * 1024 * 1024


def _bucket_thresholds():
    exact = REL_BUCKETS // 2
    n = np.arange(0, 4 * REL_MAX_DIST)
    large = exact + (np.log(np.maximum(n, 1).astype(np.float32) / np.float32(exact))
                     / np.float32(math.log(REL_MAX_DIST / exact))
                     * np.float32(REL_BUCKETS - exact)).astype(np.int32)
    large = np.minimum(large, REL_BUCKETS - 1)
    bucket = np.where(n < exact, n, large)
    return [int(n[np.argmax(bucket >= b)]) for b in range(REL_BUCKETS)]


_BUCKET_START = _bucket_thresholds()
_FAR_DIST = _BUCKET_START[REL_BUCKETS - 1]


def _rel_bias_tile(rel, rb_ref, h):
    out = jnp.full(rel.shape, rb_ref[h], F32)
    for b in range(1, REL_BUCKETS):
        out = jnp.where(rel >= _BUCKET_START[b], rb_ref[b * HEADS + h], out)
    return out


def _rmsnorm_rows(x, g):
    return x * lax.rsqrt(jnp.mean(x * x, axis=-1, keepdims=True) + EPS) * g


def _cparams(sem):
    return pltpu.CompilerParams(dimension_semantics=sem, vmem_limit_bytes=VMEM_LIMIT)


def _inproj_kernel(x_ref, g_ref, w_ref, q_ref, k_ref, v_ref, u_ref, qb_ref, kb_ref, vtb_ref):
    xn = _rmsnorm_rows(x_ref[...], g_ref[...])
    proj = jnp.dot(xn.astype(BF16), w_ref[...], preferred_element_type=F32)
    q = proj[:, 0 * ATT_W:1 * ATT_W]
    k = proj[:, 1 * ATT_W:2 * ATT_W]
    v = proj[:, 2 * ATT_W:3 * ATT_W]
    q_ref[...] = q
    tm = q.shape[0]
    for h in range(HEADS):
        k_ref[pl.ds(h, tm, stride=HEADS), :] = k[:, h * HEAD_W:(h + 1) * HEAD_W]
        v_ref[pl.ds(h, tm, stride=HEADS), :] = v[:, h * HEAD_W:(h + 1) * HEAD_W]
    u_ref[...] = proj[:, 3 * ATT_W:4 * ATT_W]
    qb_ref[...] = (q * (QK_DIM ** -0.5)).astype(BF16)
    kb_ref[...] = k.astype(BF16)
    vtb_ref[...] = v.T.astype(BF16)


def _inproj(x, g, w_bf, tm):
    t, d = x.shape
    n_out = w_bf.shape[1]
    assert n_out == 4 * ATT_W and t % tm == 0
    out = jax.ShapeDtypeStruct((t, ATT_W), F32)
    out_kv = jax.ShapeDtypeStruct((t * HEADS, HEAD_W), F32)
    out_bf = jax.ShapeDtypeStruct((t, ATT_W), BF16)
    row = pl.BlockSpec((tm, ATT_W), lambda i: (i, 0))
    row_kv = pl.BlockSpec((tm * HEADS, HEAD_W), lambda i: (i, 0))
    return pl.pallas_call(
        _inproj_kernel,
        grid=(t // tm,),
        in_specs=[pl.BlockSpec((tm, d), lambda i: (i, 0)),
                  pl.BlockSpec((1, d), lambda i: (0, 0)),
                  pl.BlockSpec((d, n_out), lambda i: (0, 0))],
        out_specs=[row, row_kv, row_kv, row, row, row, pl.BlockSpec((ATT_W, tm), lambda i: (0, i))],
        out_shape=[out, out_kv, out_kv, out, out_bf, out_bf,
                   jax.ShapeDtypeStruct((ATT_W, t), BF16)],
        compiler_params=_cparams(("arbitrary",)),
        name="inproj",
    )(x, g, w_bf)


def _lambda_full(lq1, lk1, lq2, lk2, lam_init):
    return (jnp.exp(jnp.sum(lq1 * lk1, axis=-1, keepdims=True))
            - jnp.exp(jnp.sum(lq2 * lk2, axis=-1, keepdims=True)) + lam_init)


def _head_out(acc1, l1, acc2, l2, lam, subln_g, lam_init):
    o = acc1 / l1 - lam * (acc2 / l2)
    return _rmsnorm_rows(o, subln_g) * (1.0 - lam_init)


def _attn_prompt_kernel(rb_ref, q_ref, k_ref, vt_ref, lq1_ref, lk1_ref, lq2_ref, lk2_ref, sg_ref,
                        o_ref, bias_ref, ve_ref, m_ref, acc_ref, *, tb, lam_init):
    b, qi, kj = pl.program_id(0), pl.program_id(1), pl.program_id(2)

    @pl.when((b == 0) & (qi == 0) & (kj == 0))
    def _build_constants():
        key = lax.broadcasted_iota(jnp.int32, (tb, tb), 0)
        qry = lax.broadcasted_iota(jnp.int32, (tb, tb), 1)
        rel = qry - key
        for h in range(HEADS):
            diag = _rel_bias_tile(rel, rb_ref, h)
            bias_ref[0, h] = jnp.where(rel >= 0, diag, NEG_INF)
            bias_ref[1, h] = _rel_bias_tile(rel + tb, rb_ref, h)
            bias_ref[2, h] = jnp.full((tb, tb), rb_ref[(REL_BUCKETS - 1) * HEADS + h], F32)
            ve_ref[h, HEAD_W:HEAD_W + ONES_ROWS, :] = jnp.ones((ONES_ROWS, tb), ve_ref.dtype)

    @pl.when(kj == 0)
    def _init():
        m_ref[...] = jnp.full(m_ref.shape, -jnp.inf, F32)
        acc_ref[...] = jnp.zeros(acc_ref.shape, F32)

    @pl.when(kj <= qi)
    def _step():
        sel = jnp.where(kj == qi, 0, jnp.where(kj == qi - 1, 1, 2))
        for h in range(HEADS):
            ve_ref[h, 0:HEAD_W, :] = vt_ref[h * HEAD_W:(h + 1) * HEAD_W, :]
        for h in range(HEADS):
            for mp in range(2):
                c0 = h * HEAD_W + mp * QK_DIM
                idx = h * 2 + mp
                st = lax.dot_general(k_ref[:, c0:c0 + QK_DIM], q_ref[:, c0:c0 + QK_DIM],
                                     (((1,), (1,)), ((), ())), preferred_element_type=F32)
                st = st + bias_ref[sel, h]
                m_old = m_ref[idx]
                m_new = jnp.maximum(m_old, jnp.max(st, axis=0, keepdims=True))
                alpha = jnp.exp(m_old - m_new)
                p = jnp.exp(st - m_new).astype(BF16)
                acc_ref[idx] = alpha * acc_ref[idx] + jnp.dot(
                    ve_ref[h], p, preferred_element_type=F32)
                m_ref[idx] = m_new

    @pl.when(kj == qi)
    def _finish():
        lam = _lambda_full(lq1_ref[...], lk1_ref[...], lq2_ref[...], lk2_ref[...], lam_init)
        for h in range(HEADS):
            a1, a2 = acc_ref[2 * h], acc_ref[2 * h + 1]
            o = (a1[0:HEAD_W] / a1[HEAD_W:HEAD_W + 1]
                 - lam * (a2[0:HEAD_W] / a2[HEAD_W:HEAD_W + 1]))
            ms = jnp.mean(o * o, axis=0, keepdims=True)
            on = o * lax.rsqrt(ms + EPS) * sg_ref[...] * (1.0 - lam_init)
            o_ref[:, h * HEAD_W:(h + 1) * HEAD_W] = on.T


def _attn_prompt(q, k, vt, rb_flat, lq1, lk1, lq2, lk2, subln_col, batch, seq, tb, lam_init):
    t = q.shape[0]
    assert t == batch * seq and seq % tb == 0 and tb >= _FAR_DIST and q.dtype == BF16
    nq = seq // tb
    small = lambda n: pl.BlockSpec((1, n), lambda b, i, j: (0, 0))
    k_spec = pl.BlockSpec((tb, ATT_W), lambda b, i, j: (b * nq + jnp.minimum(j, i), 0))
    vt_spec = pl.BlockSpec((ATT_W, tb), lambda b, i, j: (0, b * nq + jnp.minimum(j, i)))
    q_spec = pl.BlockSpec((tb, ATT_W), lambda b, i, j: (b * nq + i, 0))
    return pl.pallas_call(
        functools.partial(_attn_prompt_kernel, tb=tb, lam_init=lam_init),
        grid=(batch, nq, nq),
        in_specs=[pl.BlockSpec(memory_space=pltpu.SMEM),
                  q_spec, k_spec, vt_spec,
                  small(QK_DIM), small(QK_DIM), small(QK_DIM), small(QK_DIM),
                  pl.BlockSpec((HEAD_W, 1), lambda b, i, j: (0, 0))],
        out_specs=q_spec,
        out_shape=jax.ShapeDtypeStruct((t, ATT_W), F32),
        scratch_shapes=[pltpu.VMEM((3, HEADS, tb, tb), F32),
                        pltpu.VMEM((HEADS, HEAD_W + ONES_ROWS, tb), BF16),
                        pltpu.VMEM((2 * HEADS, 1, tb), F32),
                        pltpu.VMEM((2 * HEADS, HEAD_W + ONES_ROWS, tb), F32)],
        compiler_params=_cparams(("arbitrary", "arbitrary", "arbitrary")),
        name="attn_prompt",
    )(rb_flat, q, k, vt, lq1, lk1, lq2, lk2, subln_col)


QROWS = 8


def _attn_sample_kernel(pt_ref, rb_ref, q_ref, kn_ref, vn_ref, *rest,
                        n_steps, pp, page, dec, lam_init):
    del pt_ref
    k_refs, v_refs = rest[:pp], rest[pp:2 * pp]
    (lq1_ref, lk1_ref, lq2_ref, lk2_ref, sg_ref, o_ref,
     q8_ref, qcat_ref, bias_ref, m_ref, l_ref, acc_ref) = rest[2 * pp:]
    b, p = pl.program_id(0), pl.program_id(1)
    hrows = 2 * QROWS
    rows = HEADS * hrows
    prow = page * HEADS
    width = pp * prow
    past_len = n_steps * pp * page
    row_i = lax.broadcasted_iota(jnp.int32, (rows, 1), 0) % QROWS

    def per_head(fn):
        return jnp.concatenate([fn(h) for h in range(HEADS)], axis=0)

    @pl.when(b == 0)
    def _build_bias():
        col = lax.broadcasted_iota(jnp.int32, (hrows, width), 1)
        k_pos = p * (pp * page) + col // HEADS
        rel = past_len + lax.broadcasted_iota(jnp.int32, (hrows, width), 0) % QROWS - k_pos
        bias_ref[p] = per_head(lambda h: jnp.where(col % HEADS == h,
                                                   _rel_bias_tile(rel, rb_ref, h), NEG_INF))

    @pl.when(p == 0)
    def _init():
        q8_ref[...] = jnp.zeros(q8_ref.shape, F32)
        q8_ref[0:dec, :] = q_ref[0] * (QK_DIM ** -0.5)
        col = lax.broadcasted_iota(jnp.int32, (QROWS, HEAD_W), 1)
        for h in range(HEADS):
            qh = q8_ref[:, h * HEAD_W:(h + 1) * HEAD_W]
            for mp in range(2):
                r0 = h * hrows + mp * QROWS
                qcat_ref[r0:r0 + QROWS, :] = jnp.where(col // QK_DIM == mp, qh, 0.0)
        m_ref[...] = jnp.full(m_ref.shape, -jnp.inf, F32)
        l_ref[...] = jnp.zeros(l_ref.shape, F32)
        acc_ref[...] = jnp.zeros(acc_ref.shape, F32)

    qcat = qcat_ref[...]
    qcat_bf = qcat.astype(BF16)
    s = jnp.concatenate(
        [lax.dot_general(qcat_bf, k_refs[r][...].astype(BF16), (((1,), (1,)), ((), ())),
                         preferred_element_type=F32) for r in range(pp)], axis=1)
    s = s + bias_ref[p]
    m_old = m_ref[...]
    m_new = jnp.maximum(m_old, jnp.max(s, axis=-1, keepdims=True))
    alpha = jnp.exp(m_old - m_new)
    pr = jnp.exp(s - m_new)
    l_ref[...] = alpha * l_ref[...] + jnp.sum(pr, axis=-1, keepdims=True)
    pv = jnp.dot(pr[:, 0:prow].astype(BF16), v_refs[0][...].astype(BF16),
                 preferred_element_type=F32)
    for r in range(1, pp):
        pv = pv + jnp.dot(pr[:, r * prow:(r + 1) * prow].astype(BF16),
                          v_refs[r][...].astype(BF16), preferred_element_type=F32)
    acc_ref[...] = alpha * acc_ref[...] + pv
    m_ref[...] = m_new

    @pl.when(p == n_steps - 1)
    def _finish():
        def head_row(ref, j, h):
            return ref[0, j * HEADS + h:j * HEADS + h + 1, :]

        s_new = []
        for j in range(dec):
            sj = per_head(lambda h, j=j: jnp.sum(
                qcat[h * hrows:(h + 1) * hrows] * head_row(kn_ref, j, h), axis=-1, keepdims=True)
                + _rel_bias_tile(row_i[0:hrows] - j, rb_ref, h))
            s_new.append(jnp.where(row_i >= j, sj, NEG_INF))
        m_old2 = m_ref[...]
        m_fin = m_old2
        for sj in s_new:
            m_fin = jnp.maximum(m_fin, sj)
        alpha2 = jnp.exp(m_old2 - m_fin)
        l_fin = alpha2 * l_ref[...]
        acc_fin = alpha2 * acc_ref[...]
        for j, sj in enumerate(s_new):
            pj = jnp.exp(sj - m_fin)
            l_fin = l_fin + pj
            acc_fin = acc_fin + pj * per_head(
                lambda h, j=j: jnp.broadcast_to(head_row(vn_ref, j, h), (hrows, HEAD_W)))
        lam = _lambda_full(lq1_ref[...], lk1_ref[...], lq2_ref[...], lk2_ref[...], lam_init)
        for h in range(HEADS):
            r1, r2 = h * hrows, h * hrows + QROWS
            oh = _head_out(acc_fin[r1:r1 + QROWS], l_fin[r1:r1 + QROWS],
                           acc_fin[r2:r2 + QROWS], l_fin[r2:r2 + QROWS],
                           lam, sg_ref[...], lam_init)
            o_ref[0, :, h * HEAD_W:(h + 1) * HEAD_W] = oh[0:dec]


def _attn_sample(q, k_new, v_new, cache_k, cache_v, layer, page_table, rb_flat,
                 lq1, lk1, lq2, lk2, subln_g, dec, lam_init, pp):
    nseq, n_pages = page_table.shape
    depth, n_phys, page = cache_k.shape[:3]
    assert dec <= QROWS and page >= _FAR_DIST and n_pages % pp == 0
    n_steps = n_pages // pp
    q3 = q.reshape(nseq, dec, ATT_W)
    kn3 = k_new.reshape(nseq, dec * HEADS, HEAD_W)
    vn3 = v_new.reshape(nseq, dec * HEADS, HEAD_W)
    new_spec = pl.BlockSpec((1, dec * HEADS, HEAD_W), lambda b, p, pt: (b, 0, 0))
    prow = page * HEADS
    kc = cache_k.reshape(depth * n_phys * prow, HEAD_W)
    vc = cache_v.reshape(depth * n_phys * prow, HEAD_W)
    rows = 2 * HEADS * QROWS
    seq_spec = pl.BlockSpec((1, dec, ATT_W), lambda b, p, pt: (b, 0, 0))

    def page_spec(r):
        return pl.BlockSpec((prow, HEAD_W),
                            lambda b, p, pt: (layer * n_phys + pt[b, p * pp + r], 0))

    pages = [page_spec(r) for r in range(pp)]
    small = lambda n: pl.BlockSpec((1, n), lambda b, p, pt: (0, 0))
    grid_spec = pltpu.PrefetchScalarGridSpec(
        num_scalar_prefetch=1,
        grid=(nseq, n_steps),
        in_specs=[pl.BlockSpec(memory_space=pltpu.SMEM), seq_spec, new_spec, new_spec]
        + pages + pages
        + [small(QK_DIM), small(QK_DIM), small(QK_DIM), small(QK_DIM), small(HEAD_W)],
        out_specs=seq_spec,
        scratch_shapes=[pltpu.VMEM((QROWS, ATT_W), F32),
                        pltpu.VMEM((rows, HEAD_W), F32),
                        pltpu.VMEM((n_steps, rows, pp * prow), F32),
                        pltpu.VMEM((rows, 1), F32),
                        pltpu.VMEM((rows, 1), F32),
                        pltpu.VMEM((rows, HEAD_W), F32)])
    out = pl.pallas_call(
        functools.partial(_attn_sample_kernel, n_steps=n_steps, pp=pp, page=page, dec=dec,
                          lam_init=lam_init),
        grid_spec=grid_spec,
        out_shape=jax.ShapeDtypeStruct((nseq, dec, ATT_W), F32),
        compiler_params=_cparams(("arbitrary", "arbitrary")),
        name="attn_sample",
    )(page_table, rb_flat, q3, kn3, vn3, *([kc] * pp), *([vc] * pp), lq1, lk1, lq2, lk2, subln_g)
    return out.reshape(nseq * dec, ATT_W)


def _pool_groups(window_sum, u_of, cnt_of, pw_ref, ps_ref):
    outs = []
    for g, w in enumerate(POOL_WINDOWS):
        d = window_sum(g, w) / cnt_of(w) - u_of(g)
        outs.append(jnp.dot(d.astype(BF16), pw_ref[g], preferred_element_type=F32))
    return jnp.concatenate(outs, axis=-1) * ps_ref[...]


def _mix_prompt_kernel(x_ref, a_ref, u_ref, halo_ref, pw_ref, ps_ref, wo_ref, o_ref, z_ref,
                       *, tm, tiles_per_seq):
    i = pl.program_id(0)
    t_in_seq = i % tiles_per_seq
    z_ref[0:POOL_HALO, :] = jnp.where(t_in_seq == 0, 0.0, halo_ref[...])
    z_ref[POOL_HALO:POOL_HALO + tm, :] = u_ref[...]
    pos = t_in_seq * tm + lax.broadcasted_iota(jnp.int32, (tm, 1), 0)

    def window_sum(g, w):
        cs = slice(g * POOL_GW, (g + 1) * POOL_GW)
        acc = z_ref[POOL_HALO:POOL_HALO + tm, cs]
        for k in range(1, w):
            acc = acc + z_ref[POOL_HALO - k:POOL_HALO - k + tm, cs]
        return acc

    pool = _pool_groups(
        window_sum,
        lambda g: u_ref[:, g * POOL_GW:(g + 1) * POOL_GW],
        lambda w: jnp.minimum(pos + 1, w).astype(F32),
        pw_ref, ps_ref)
    mixed = jnp.concatenate([a_ref[...], pool], axis=-1).astype(BF16)
    o_ref[...] = x_ref[...] + jnp.dot(mixed, wo_ref[...], preferred_element_type=F32)


def _mix_prompt(x, attn, u, pool_w_bf, pool_scale, w_out_bf, seq, tm):
    t, d = x.shape
    pw_cols = u.shape[1]
    assert seq % tm == 0 and tm % POOL_HALO == 0
    halo_blocks = tm // POOL_HALO
    return pl.pallas_call(
        functools.partial(_mix_prompt_kernel, tm=tm, tiles_per_seq=seq // tm),
        grid=(t // tm,),
        in_specs=[pl.BlockSpec((tm, d), lambda i: (i, 0)),
                  pl.BlockSpec((tm, ATT_W), lambda i: (i, 0)),
                  pl.BlockSpec((tm, pw_cols), lambda i: (i, 0)),
                  pl.BlockSpec((POOL_HALO, pw_cols),
                               lambda i: (jnp.maximum(i * halo_blocks - 1, 0), 0)),
                  pl.BlockSpec(pool_w_bf.shape, lambda i: (0, 0, 0)),
                  pl.BlockSpec((1, pw_cols), lambda i: (0, 0)),
                  pl.BlockSpec(w_out_bf.shape, lambda i: (0, 0))],
        out_specs=pl.BlockSpec((tm, d), lambda i: (i, 0)),
        out_shape=jax.ShapeDtypeStruct((t, d), F32),
        scratch_shapes=[pltpu.VMEM((POOL_HALO + tm, pw_cols), F32)],
        compiler_params=_cparams(("arbitrary",)),
        name="mix_prompt",
    )(x, attn, u, u, pool_w_bf, pool_scale, w_out_bf)


def _mix_sample_kernel(x_ref, a_ref, u_ref, st_ref, pw_ref, ps_ref, wo_ref, o_ref,
                       *, dec, start_pos):
    def z_row(r):
        return st_ref[r] if r < POOL_PAD else u_ref[r - POOL_PAD]

    for i in range(dec):
        def window_sum(g, w, i=i):
            cs = slice(g * POOL_GW, (g + 1) * POOL_GW)
            acc = z_row(POOL_PAD + i)[:, cs]
            for k in range(1, w):
                acc = acc + z_row(POOL_PAD + i - k)[:, cs]
            return acc

        pool = _pool_groups(
            window_sum,
            lambda g, i=i: u_ref[i][:, g * POOL_GW:(g + 1) * POOL_GW],
            lambda w, i=i: float(min(start_pos + i + 1, w)),
            pw_ref, ps_ref)
        mixed = jnp.concatenate([a_ref[i], pool], axis=-1).astype(BF16)
        o_ref[i] = x_ref[i] + jnp.dot(mixed, wo_ref[...], preferred_element_type=F32)


def _mix_sample(x3, attn3, u3, state3, pool_w_bf, pool_scale, w_out_bf, start_pos):
    dec, nseq, d = x3.shape
    full = lambda a: pl.BlockSpec(a.shape, lambda i: (0,) * a.ndim)
    args = (x3, attn3, u3, state3, pool_w_bf, pool_scale, w_out_bf)
    return pl.pallas_call(
        functools.partial(_mix_sample_kernel, dec=dec, start_pos=start_pos),
        grid=(1,),
        in_specs=[full(a) for a in args],
        out_specs=full(x3),
        out_shape=jax.ShapeDtypeStruct((dec, nseq, d), F32),
        compiler_params=_cparams(("arbitrary",)),
        name="mix_sample",
    )(*args)


def _erf_gelu(x):
    return 0.5 * x * (1.0 + lax.erf(x * (2.0 ** -0.5)))


def _top_rows(x, n, with_rank=False):
    rows = []
    rank = jnp.full(x.shape, float(n), F32) if with_rank else None
    for r in range(n):
        m = jnp.max(x, axis=0, keepdims=True)
        rows.append(m)
        hit = x == m
        if with_rank:
            rank = jnp.where(hit, float(r), rank)
        if r + 1 < n:
            x = jnp.where(hit, -jnp.inf, x)
    return (rows, rank) if with_rank else rows


def _stack_rows(rows):
    n = len(rows)
    rid = lax.broadcasted_iota(jnp.int32, (n, LANES), 0)
    out = jnp.broadcast_to(rows[0], (n, LANES))
    for r in range(1, n):
        out = jnp.where(rid == r, rows[r], out)
    return out


def _peer_route_block(s1, s2):
    k = PEER_TOPK
    t1 = _top_rows(s1, k)
    t2, rank2 = _top_rows(s2, k, with_rank=True)
    t1s = _stack_rows(t1)
    t2s = _stack_rows(t2)
    half = k // 2
    rid = lax.broadcasted_iota(jnp.int32, (half, LANES), 0)
    cands = [t1s[0:half] + t2[0], t1s[half:k] + t2[0], t1s[0:half] + t2[1]]
    for q in range(2, half):
        cands.append(jnp.where(rid < k // (q + 1), t1s[0:half] + t2[q], -jnp.inf))
    cands.append(t2s[half:k] + t1[0])
    cand = jnp.concatenate(cands, axis=0)
    tau = _top_rows(cand, k)[k - 1]
    top = t1[0] + t2[0]
    z = jnp.sum(jnp.where(cand >= tau, jnp.exp(cand - top), 0.0), axis=0, keepdims=True)
    cnt = jnp.zeros(s1.shape, F32)
    for q in range(k):
        cnt = jnp.where(s1 + t2[q] >= tau, float(q + 1), cnt)
    a = jnp.exp(s1 - t1[0])
    b = jnp.exp(s2 - t2[0]) / z
    return cnt, rank2, a, b


def _peer_kernel(x_ref, g_ref, wq_ref, keys_ref, u_ref, vt_ref, o_ref,
                 xn_ref, s1_ref, s2_ref, ca_ref, rb_ref, w0_ref, w1_ref, acc_ref,
                 *, tt, ec):
    j = pl.program_id(1)
    n_chunks = pl.num_programs(1)
    nsb = tt // LANES
    c_per = ec // PEER_KEYS
    pk = BF16_SUBLANES

    n_pieces = nsb * PEER_KEYS // GATE_ROWS

    def gates(chunk, w_ref, pieces=range(n_pieces)):
        zero = jnp.zeros((pk, LANES), BF16)
        c0 = pl.multiple_of(chunk * c_per, c_per)
        gpp = GATE_ROWS // pk
        for piece in pieces:
            sb, r0 = piece // (PEER_KEYS // GATE_ROWS), (piece % (PEER_KEYS // GATE_ROWS)) * GATE_ROWS
            w = [[zero] * gpp for _ in range(c_per)]
            for h in range(PEER_HEADS):
                cnt = [jnp.broadcast_to(ca_ref[h, sb, pl.ds(c0 + cl, 1), :], (pk, LANES)).astype(BF16)
                       for cl in range(c_per)]
                a = [jnp.broadcast_to(ca_ref[h, sb, pl.ds(A_OFF + c0 + cl, 1), :],
                                      (pk, LANES)).astype(BF16) for cl in range(c_per)]
                for g in range(gpp):
                    lo = r0 + g * pk
                    rank = rb_ref[h, sb, lo:lo + pk, :]
                    b = rb_ref[h, sb, B_OFF + lo:B_OFF + lo + pk, :]
                    for cl in range(c_per):
                        w[cl][g] = w[cl][g] + jnp.where(rank < cnt[cl], b * a[cl], zero)
            for cl in range(c_per):
                for g in range(gpp):
                    e0 = cl * PEER_KEYS + r0 + g * pk
                    w_ref[e0:e0 + pk, sb * LANES:(sb + 1) * LANES] = w[cl][g]

    @pl.when(j == 0)
    def _route():
        xn = _rmsnorm_rows(x_ref[...], g_ref[...]).astype(BF16)
        xn_ref[...] = xn

        def head_scores(h, carry):
            qh = jnp.dot(xn_ref[...], wq_ref[h], preferred_element_type=F32)
            half = qh.shape[1] // 2
            for part, dst in ((0, s1_ref), (1, s2_ref)):
                qp = qh[:, part * half:(part + 1) * half].astype(BF16)
                st = lax.dot_general(keys_ref[h, part], qp, (((1,), (1,)), ((), ())),
                                     preferred_element_type=F32)
                for sb in range(nsb):
                    dst[h, sb] = st[:, sb * LANES:(sb + 1) * LANES]
            return carry

        lax.fori_loop(0, PEER_HEADS, head_scores, 0)

        def route(idx, carry):
            h, sb = idx // nsb, idx % nsb
            cnt, rank2, a, b = _peer_route_block(s1_ref[h, sb], s2_ref[h, sb])
            ca_ref[h, sb, 0:PEER_KEYS, :] = cnt
            ca_ref[h, sb, A_OFF:A_OFF + PEER_KEYS, :] = a
            rb_ref[h, sb, 0:PEER_KEYS, :] = rank2.astype(BF16)
            rb_ref[h, sb, B_OFF:B_OFF + PEER_KEYS, :] = b.astype(BF16)
            return carry

        lax.fori_loop(0, PEER_HEADS * nsb, route, 0)
        acc_ref[...] = jnp.zeros(acc_ref.shape, F32)

        gates(0, w0_ref)

    def step(w_cur_ref, w_next_ref):
        nxt = (j + 1) % n_chunks
        d = xn_ref.shape[1]
        kd, ke, cg = d // PEER_KSPLIT, ec // PEER_KSPLIT, n_pieces // (2 * PEER_KSPLIT)
        at = None
        for kt in range(PEER_KSPLIT):
            part = lax.dot_general(u_ref[:, kt * kd:(kt + 1) * kd], xn_ref[:, kt * kd:(kt + 1) * kd],
                                   (((1,), (1,)), ((), ())), preferred_element_type=F32)
            at = part if at is None else at + part
            gates(nxt, w_next_ref, range(kt * cg, (kt + 1) * cg))
        out = None
        for kt in range(PEER_KSPLIT):
            rows = slice(kt * ke, (kt + 1) * ke)
            p = w_cur_ref[rows, :] * _erf_gelu(at[rows]).astype(BF16)
            part = jnp.dot(vt_ref[:, rows], p, preferred_element_type=F32)
            out = part if out is None else out + part
            gates(nxt, w_next_ref, range((PEER_KSPLIT + kt) * cg, (PEER_KSPLIT + kt + 1) * cg))
        acc_ref[...] += out

    @pl.when(j % 2 == 0)
    def _even():
        step(w0_ref, w1_ref)

    @pl.when(j % 2 == 1)
    def _odd():
        step(w1_ref, w0_ref)

    @pl.when(j == n_chunks - 1)
    def _finish():
        o_ref[...] = x_ref[...] + acc_ref[...].T


def _peer(x, g, wq_heads_bf, keys_bf, u_bf, vt_bf, tt, ec):
    t, d = x.shape
    n_exp = u_bf.shape[0]
    assert t % tt == 0 and n_exp % ec == 0 and tt % LANES == 0 and ec % PEER_KEYS == 0
    assert n_exp == PEER_KEYS * PEER_KEYS
    nsb = tt // LANES
    assert (nsb * PEER_KEYS // GATE_ROWS) % (2 * PEER_KSPLIT) == 0 and d % PEER_KSPLIT == 0
    blk = lambda dt: pltpu.VMEM((PEER_HEADS, nsb, PEER_KEYS, LANES), dt)
    return pl.pallas_call(
        functools.partial(_peer_kernel, tt=tt, ec=ec),
        grid=(t // tt, n_exp // ec),
        in_specs=[pl.BlockSpec((tt, d), lambda i, j: (i, 0)),
                  pl.BlockSpec((1, d), lambda i, j: (0, 0)),
                  pl.BlockSpec(wq_heads_bf.shape, lambda i, j: (0, 0, 0)),
                  pl.BlockSpec(keys_bf.shape, lambda i, j: (0, 0, 0, 0)),
                  pl.BlockSpec((ec, d), lambda i, j: (j, 0)),
                  pl.BlockSpec((d, ec), lambda i, j: (0, j))],
        out_specs=pl.BlockSpec((tt, d), lambda i, j: (i, 0)),
        out_shape=jax.ShapeDtypeStruct((t, d), F32),
        scratch_shapes=[pltpu.VMEM((tt, d), BF16),
                        blk(F32), blk(F32),
                        pltpu.VMEM((PEER_HEADS, nsb, A_OFF + PEER_KEYS, LANES), F32),
                        pltpu.VMEM((PEER_HEADS, nsb, B_OFF + PEER_KEYS, LANES), BF16),
                        pltpu.VMEM((ec, tt), BF16),
                        pltpu.VMEM((ec, tt), BF16),
                        pltpu.VMEM((d, tt), F32)],
        compiler_params=_cparams(("arbitrary", "arbitrary")),
        name="peer",
    )(x, g, wq_heads_bf, keys_bf, u_bf, vt_bf)


def _tail_kernel(x_ref, p_ref, gpl_ref, wg_ref, wpl_ref, gf_ref, o_ref):
    x = x_ref[...]
    xn = _rmsnorm_rows(x, gpl_ref[...]).astype(BF16)
    gate = jax.nn.sigmoid(jnp.dot(xn, wg_ref[...], preferred_element_type=F32))
    emb = jnp.dot(p_ref[...].astype(BF16), wpl_ref[...], preferred_element_type=F32)
    o_ref[...] = _rmsnorm_rows(x + gate * emb, gf_ref[...])


def _tail(x, p, g_pl, w_gate_bf, w_pl_bf, g_final, tm):
    t, d = x.shape
    pd = p.shape[1]
    return pl.pallas_call(
        _tail_kernel,
        grid=(t // tm,),
        in_specs=[pl.BlockSpec((tm, d), lambda i: (i, 0)),
                  pl.BlockSpec((tm, pd), lambda i: (i, 0)),
                  pl.BlockSpec((1, d), lambda i: (0, 0)),
                  pl.BlockSpec((d, d), lambda i: (0, 0)),
                  pl.BlockSpec((pd, d), lambda i: (0, 0)),
                  pl.BlockSpec((1, d), lambda i: (0, 0))],
        out_specs=pl.BlockSpec((tm, d), lambda i: (i, 0)),
        out_shape=jax.ShapeDtypeStruct((t, d), F32),
        compiler_params=_cparams(("arbitrary",)),
        name="tail",
    )(x, p, g_pl, w_gate_bf, w_pl_bf, g_final)


def _row_tile(t, want):
    tm = min(t, want)
    assert t % tm == 0
    return tm


def kernel(x_prompt, x_sample, cache_k, cache_v, page_table, state_pool, p_prompt, p_sample, norm_attn_g, w_in, lambda_q1, lambda_k1, lambda_q2, lambda_k2, subln_g, rel_bias, pool_w, pool_scale, w_out, norm_ffn_g, peer_wq, peer_keys, peer_u, peer_v, norm_pl_g, w_pl, w_pl_gate, final_norm_g):
    depth = w_in.shape[0]
    assert depth == 1
    batch, seq, d = x_prompt.shape
    nseq, dec, _ = x_sample.shape
    n_pages = page_table.shape[1]
    page = cache_k.shape[2]
    past_len = n_pages * page
    lam_init = 0.8 - 0.6 * math.exp(-0.3 * 0)
    i = 0

    row = lambda a: a.reshape(1, -1)
    w_in_bf = w_in[i].astype(BF16)
    pool_w_bf = pool_w[i].astype(BF16)
    w_out_bf = w_out[i].astype(BF16)
    qd = peer_wq.shape[2] // PEER_HEADS
    wq_heads_bf = peer_wq[i].reshape(d, PEER_HEADS, qd).transpose(1, 0, 2).astype(BF16)
    keys_bf = peer_keys[i].astype(BF16)
    u_bf = peer_u[i].astype(BF16)
    vt_bf = peer_v[i].T.astype(BF16)
    w_gate_bf = w_pl_gate[i].astype(BF16)
    w_pl_bf = w_pl[i].astype(BF16)
    rb_flat = rel_bias.reshape(-1)
    lam_rows = (row(lambda_q1[i]), row(lambda_k1[i]), row(lambda_q2[i]), row(lambda_k2[i]))
    sg = row(subln_g[i])

    tp = batch * seq
    xp = x_prompt.reshape(tp, d)
    _, k, v, u, qb, kb, vtb = _inproj(xp, row(norm_attn_g[i]), w_in_bf, _row_tile(tp, 512))
    attn = _attn_prompt(qb, kb, vtb, rb_flat, *lam_rows, subln_g[i].reshape(-1, 1),
                        batch, seq, min(seq, 512), lam_init)
    x1 = _mix_prompt(xp, attn, u, pool_w_bf, row(pool_scale[i]), w_out_bf, seq, min(seq, 512))
    x2 = _peer(x1, row(norm_ffn_g[i]), wq_heads_bf, keys_bf, u_bf, vt_bf,
               _row_tile(tp, 512), min(u_bf.shape[0], 1024))
    y_prompt = _tail(x2, p_prompt[i].reshape(tp, -1), row(norm_pl_g[i]), w_gate_bf, w_pl_bf,
                     row(final_norm_g), _row_tile(tp, 512)).reshape(batch, seq, d)
    new_k_prompt = k.reshape(1, batch, seq, HEADS, HEAD_W)
    new_v_prompt = v.reshape(1, batch, seq, HEADS, HEAD_W)
    new_pool_prompt = u.reshape(batch, seq, -1)[:, seq - POOL_PAD:, :][None]

    ts = nseq * dec
    xs = x_sample.reshape(ts, d)
    qs, ks, vs, us, _, _, _ = _inproj(xs, row(norm_attn_g[i]), w_in_bf, _row_tile(ts, 512))
    attn_s = _attn_sample(qs, ks, vs, cache_k, cache_v, i, page_table, rb_flat,
                          *lam_rows, sg, dec, lam_init, math.gcd(n_pages, 8))
    tok_major = lambda a: a.reshape(nseq, dec, -1).transpose(1, 0, 2)
    x1s = _mix_sample(tok_major(xs), tok_major(attn_s), tok_major(us),
                      state_pool[i].transpose(1, 0, 2), pool_w_bf, row(pool_scale[i]),
                      w_out_bf, past_len)
    x1s = x1s.transpose(1, 0, 2).reshape(ts, d)
    x2s = _peer(x1s, row(norm_ffn_g[i]), wq_heads_bf, keys_bf, u_bf, vt_bf,
                _row_tile(ts, 512), min(u_bf.shape[0], 1024))
    y_sample = _tail(x2s, p_sample[i].reshape(ts, -1), row(norm_pl_g[i]), w_gate_bf, w_pl_bf,
                     row(final_norm_g), _row_tile(ts, 512)).reshape(nseq, dec, d)
    new_k_sample = ks.reshape(1, nseq, dec, HEADS, HEAD_W)
    new_v_sample = vs.reshape(1, nseq, dec, HEADS, HEAD_W)
    new_pool_sample = jnp.concatenate(
        [state_pool[i][:, dec:, :], us.reshape(nseq, dec, -1)], axis=1)[None]

    return (y_prompt, y_sample, new_k_prompt, new_v_prompt, new_pool_prompt,
            new_k_sample, new_v_sample, new_pool_sample)
```

```python
import functools
import math

import jax
import jax.numpy as jnp
import numpy as np
from jax import lax
from jax.experimental import pallas as pl
from jax.experimental.pallas import tpu as pltpu

F32 = jnp.float32
BF16 = jnp.bfloat16

EPS = 1e-6
NEG_INF = -1e30
HEADS = 4
QK_DIM = 64
HEAD_W = 2 * QK_DIM
ATT_W = HEADS * HEAD_W
POOL_WINDOWS = (2, 4, 8, 16)
POOL_GW = 128
POOL_PAD = 15
POOL_HALO = 16
REL_BUCKETS = 32
REL_MAX_DIST = 128
PEER_HEADS = 8
PEER_KEYS = 128
PEER_TOPK = 16
LANES = 128
BF16_SUBLANES = 16
GATE_ROWS = 32
A_OFF = PEER_KEYS + 8
B_OFF = PEER_KEYS + BF16_SUBLANES
ONES_ROWS = 16
VMEM_LIMIT = 56 * 1024 * 1024


def _bucket_thresholds():
    exact = REL_BUCKETS // 2
    n = np.arange(0, 4 * REL_MAX_DIST)
    large = exact + (np.log(np.maximum(n, 1).astype(np.float32) / np.float32(exact))
                     / np.float32(math.log(REL_MAX_DIST / exact))
                     * np.float32(REL_BUCKETS - exact)).astype(np.int32)
    large = np.minimum(large, REL_BUCKETS - 1)
    bucket = np.where(n < exact, n, large)
    return [int(n[np.argmax(bucket >= b)]) for b in range(REL_BUCKETS)]


_BUCKET_START = _bucket_thresholds()
_FAR_DIST = _BUCKET_START[REL_BUCKETS - 1]


def _rel_bias_tile(rel, rb_ref, h):
    out = jnp.full(rel.shape, rb_ref[h], F32)
    for b in range(1, REL_BUCKETS):
        out = jnp.where(rel >= _BUCKET_START[b], rb_ref[b * HEADS + h], out)
    return out


def _rmsnorm_rows(x, g):
    return x * lax.rsqrt(jnp.mean(x * x, axis=-1, keepdims=True) + EPS) * g


def _cparams(sem):
    return pltpu.CompilerParams(dimension_semantics=sem, vmem_limit_bytes=VMEM_LIMIT)


def _inproj_kernel(x_ref, g_ref, w_ref, q_ref, k_ref, v_ref, u_ref, qb_ref, kb_ref, vtb_ref):
    xn = _rmsnorm_rows(x_ref[...], g_ref[...])
    proj = jnp.dot(xn.astype(BF16), w_ref[...], preferred_element_type=F32)
    q = proj[:, 0 * ATT_W:1 * ATT_W]
    k = proj[:, 1 * ATT_W:2 * ATT_W]
    v = proj[:, 2 * ATT_W:3 * ATT_W]
    q_ref[...] = q
    tm = q.shape[0]
    for h in range(HEADS):
        k_ref[pl.ds(h, tm, stride=HEADS), :] = k[:, h * HEAD_W:(h + 1) * HEAD_W]
        v_ref[pl.ds(h, tm, stride=HEADS), :] = v[:, h * HEAD_W:(h + 1) * HEAD_W]
    u_ref[...] = proj[:, 3 * ATT_W:4 * ATT_W]
    qb_ref[...] = (q * (QK_DIM ** -0.5)).astype(BF16)
    kb_ref[...] = k.astype(BF16)
    vtb_ref[...] = v.T.astype(BF16)


def _inproj(x, g, w_bf, tm):
    t, d = x.shape
    n_out = w_bf.shape[1]
    assert n_out == 4 * ATT_W and t % tm == 0
    out = jax.ShapeDtypeStruct((t, ATT_W), F32)
    out_kv = jax.ShapeDtypeStruct((t * HEADS, HEAD_W), F32)
    out_bf = jax.ShapeDtypeStruct((t, ATT_W), BF16)
    row = pl.BlockSpec((tm, ATT_W), lambda i: (i, 0))
    row_kv = pl.BlockSpec((tm * HEADS, HEAD_W), lambda i: (i, 0))
    return pl.pallas_call(
        _inproj_kernel,
        grid=(t // tm,),
        in_specs=[pl.BlockSpec((tm, d), lambda i: (i, 0)),
                  pl.BlockSpec((1, d), lambda i: (0, 0)),
                  pl.BlockSpec((d, n_out), lambda i: (0, 0))],
        out_specs=[row, row_kv, row_kv, row, row, row, pl.BlockSpec((ATT_W, tm), lambda i: (0, i))],
        out_shape=[out, out_kv, out_kv, out, out_bf, out_bf,
                   jax.ShapeDtypeStruct((ATT_W, t), BF16)],
        compiler_params=_cparams(("arbitrary",)),
        name="inproj",
    )(x, g, w_bf)


def _lambda_full(lq1, lk1, lq2, lk2, lam_init):
    return (jnp.exp(jnp.sum(lq1 * lk1, axis=-1, keepdims=True))
            - jnp.exp(jnp.sum(lq2 * lk2, axis=-1, keepdims=True)) + lam_init)


def _head_out(acc1, l1, acc2, l2, lam, subln_g, lam_init):
    o = acc1 / l1 - lam * (acc2 / l2)
    return _rmsnorm_rows(o, subln_g) * (1.0 - lam_init)


def _attn_prompt_kernel(rb_ref, q_ref, k_ref, vt_ref, lq1_ref, lk1_ref, lq2_ref, lk2_ref, sg_ref,
                        o_ref, bias_ref, ve_ref, m_ref, acc_ref, *, tb, lam_init):
    b, qi, kj = pl.program_id(0), pl.program_id(1), pl.program_id(2)

    @pl.when((b == 0) & (qi == 0) & (kj == 0))
    def _build_constants():
        key = lax.broadcasted_iota(jnp.int32, (tb, tb), 0)
        qry = lax.broadcasted_iota(jnp.int32, (tb, tb), 1)
        rel = qry - key
        for h in range(HEADS):
            diag = _rel_bias_tile(rel, rb_ref, h)
            bias_ref[0, h] = jnp.where(rel >= 0, diag, NEG_INF)
            bias_ref[1, h] = _rel_bias_tile(rel + tb, rb_ref, h)
            bias_ref[2, h] = jnp.full((tb, tb), rb_ref[(REL_BUCKETS - 1) * HEADS + h], F32)
            ve_ref[h, HEAD_W:HEAD_W + ONES_ROWS, :] = jnp.ones((ONES_ROWS, tb), ve_ref.dtype)

    @pl.when(kj == 0)
    def _init():
        m_ref[...] = jnp.full(m_ref.shape, -jnp.inf, F32)
        acc_ref[...] = jnp.zeros(acc_ref.shape, F32)

    @pl.when(kj <= qi)
    def _step():
        sel = jnp.where(kj == qi, 0, jnp.where(kj == qi - 1, 1, 2))
        for h in range(HEADS):
            ve_ref[h, 0:HEAD_W, :] = vt_ref[h * HEAD_W:(h + 1) * HEAD_W, :]
        for h in range(HEADS):
            for mp in range(2):
                c0 = h * HEAD_W + mp * QK_DIM
                idx = h * 2 + mp
                st = lax.dot_general(k_ref[:, c0:c0 + QK_DIM], q_ref[:, c0:c0 + QK_DIM],
                                     (((1,), (1,)), ((), ())), preferred_element_type=F32)
                st = st + bias_ref[sel, h]
                m_old = m_ref[idx]
                m_new = jnp.maximum(m_old, jnp.max(st, axis=0, keepdims=True))
                alpha = jnp.exp(m_old - m_new)
                p = jnp.exp(st - m_new).astype(BF16)
                acc_ref[idx] = alpha * acc_ref[idx] + jnp.dot(
                    ve_ref[h], p, preferred_element_type=F32)
                m_ref[idx] = m_new

    @pl.when(kj == qi)
    def _finish():
        lam = _lambda_full(lq1_ref[...], lk1_ref[...], lq2_ref[...], lk2_ref[...], lam_init)
        for h in range(HEADS):
            a1, a2 = acc_ref[2 * h], acc_ref[2 * h + 1]
            o = (a1[0:HEAD_W] / a1[HEAD_W:HEAD_W + 1]
                 - lam * (a2[0:HEAD_W] / a2[HEAD_W:HEAD_W + 1]))
            ms = jnp.mean(o * o, axis=0, keepdims=True)
            on = o * lax.rsqrt(ms + EPS) * sg_ref[...] * (1.0 - lam_init)
            o_ref[:, h * HEAD_W:(h + 1) * HEAD_W] = on.T


def _attn_prompt(q, k, vt, rb_flat, lq1, lk1, lq2, lk2, subln_col, batch, seq, tb, lam_init):
    t = q.shape[0]
    assert t == batch * seq and seq % tb == 0 and tb >= _FAR_DIST and q.dtype == BF16
    nq = seq // tb
    small = lambda n: pl.BlockSpec((1, n), lambda b, i, j: (0, 0))
    k_spec = pl.BlockSpec((tb, ATT_W), lambda b, i, j: (b * nq + jnp.minimum(j, i), 0))
    vt_spec = pl.BlockSpec((ATT_W, tb), lambda b, i, j: (0, b * nq + jnp.minimum(j, i)))
    q_spec = pl.BlockSpec((tb, ATT_W), lambda b, i, j: (b * nq + i, 0))
    return pl.pallas_call(
        functools.partial(_attn_prompt_kernel, tb=tb, lam_init=lam_init),
        grid=(batch, nq, nq),
        in_specs=[pl.BlockSpec(memory_space=pltpu.SMEM),
                  q_spec, k_spec, vt_spec,
                  small(QK_DIM), small(QK_DIM), small(QK_DIM), small(QK_DIM),
                  pl.BlockSpec((HEAD_W, 1), lambda b, i, j: (0, 0))],
        out_specs=q_spec,
        out_shape=jax.ShapeDtypeStruct((t, ATT_W), F32),
        scratch_shapes=[pltpu.VMEM((3, HEADS, tb, tb), F32),
                        pltpu.VMEM((HEADS, HEAD_W + ONES_ROWS, tb), BF16),
                        pltpu.VMEM((2 * HEADS, 1, tb), F32),
                        pltpu.VMEM((2 * HEADS, HEAD_W + ONES_ROWS, tb), F32)],
        compiler_params=_cparams(("arbitrary", "arbitrary", "arbitrary")),
        name="attn_prompt",
    )(rb_flat, q, k, vt, lq1, lk1, lq2, lk2, subln_col)


QROWS = 8


def _attn_sample_kernel(pt_ref, rb_ref, q_ref, kn_ref, vn_ref, *rest,
                        n_steps, pp, page, dec, lam_init):
    del pt_ref
    k_refs, v_refs = rest[:pp], rest[pp:2 * pp]
    (lq1_ref, lk1_ref, lq2_ref, lk2_ref, sg_ref, o_ref,
     q8_ref, qcat_ref, bias_ref, m_ref, l_ref, acc_ref) = rest[2 * pp:]
    b, p = pl.program_id(0), pl.program_id(1)
    hrows = 2 * QROWS
    rows = HEADS * hrows
    prow = page * HEADS
    width = pp * prow
    past_len = n_steps * pp * page
    row_i = lax.broadcasted_iota(jnp.int32, (rows, 1), 0) % QROWS

    def per_head(fn):
        return jnp.concatenate([fn(h) for h in range(HEADS)], axis=0)

    @pl.when(b == 0)
    def _build_bias():
        col = lax.broadcasted_iota(jnp.int32, (hrows, width), 1)
        k_pos = p * (pp * page) + col // HEADS
        rel = past_len + lax.broadcasted_iota(jnp.int32, (hrows, width), 0) % QROWS - k_pos
        bias_ref[p] = per_head(lambda h: jnp.where(col % HEADS == h,
                                                   _rel_bias_tile(rel, rb_ref, h), NEG_INF))

    @pl.when(p == 0)
    def _init():
        q8_ref[...] = jnp.zeros(q8_ref.shape, F32)
        q8_ref[0:dec, :] = q_ref[0] * (QK_DIM ** -0.5)
        col = lax.broadcasted_iota(jnp.int32, (QROWS, HEAD_W), 1)
        for h in range(HEADS):
            qh = q8_ref[:, h * HEAD_W:(h + 1) * HEAD_W]
            for mp in range(2):
                r0 = h * hrows + mp * QROWS
                qcat_ref[r0:r0 + QROWS, :] = jnp.where(col // QK_DIM == mp, qh, 0.0)
        m_ref[...] = jnp.full(m_ref.shape, -jnp.inf, F32)
        l_ref[...] = jnp.zeros(l_ref.shape, F32)
        acc_ref[...] = jnp.zeros(acc_ref.shape, F32)

    qcat = qcat_ref[...]
    qcat_bf = qcat.astype(BF16)
    s = jnp.concatenate(
        [lax.dot_general(qcat_bf, k_refs[r][...].astype(BF16), (((1,), (1,)), ((), ())),
                         preferred_element_type=F32) for r in range(pp)], axis=1)
    s = s + bias_ref[p]
    m_old = m_ref[...]
    m_new = jnp.maximum(m_old, jnp.max(s, axis=-1, keepdims=True))
    alpha = jnp.exp(m_old - m_new)
    pr = jnp.exp(s - m_new)
    l_ref[...] = alpha * l_ref[...] + jnp.sum(pr, axis=-1, keepdims=True)
    pv = jnp.dot(pr[:, 0:prow].astype(BF16), v_refs[0][...].astype(BF16),
                 preferred_element_type=F32)
    for r in range(1, pp):
        pv = pv + jnp.dot(pr[:, r * prow:(r + 1) * prow].astype(BF16),
                          v_refs[r][...].astype(BF16), preferred_element_type=F32)
    acc_ref[...] = alpha * acc_ref[...] + pv
    m_ref[...] = m_new

    @pl.when(p == n_steps - 1)
    def _finish():
        def head_row(ref, j, h):
            return ref[0, j * HEADS + h:j * HEADS + h + 1, :]

        s_new = []
        for j in range(dec):
            sj = per_head(lambda h, j=j: jnp.sum(
                qcat[h * hrows:(h + 1) * hrows] * head_row(kn_ref, j, h), axis=-1, keepdims=True)
                + _rel_bias_tile(row_i[0:hrows] - j, rb_ref, h))
            s_new.append(jnp.where(row_i >= j, sj, NEG_INF))
        m_old2 = m_ref[...]
        m_fin = m_old2
        for sj in s_new:
            m_fin = jnp.maximum(m_fin, sj)
        alpha2 = jnp.exp(m_old2 - m_fin)
        l_fin = alpha2 * l_ref[...]
        acc_fin = alpha2 * acc_ref[...]
        for j, sj in enumerate(s_new):
            pj = jnp.exp(sj - m_fin)
            l_fin = l_fin + pj
            acc_fin = acc_fin + pj * per_head(
                lambda h, j=j: jnp.broadcast_to(head_row(vn_ref, j, h), (hrows, HEAD_W)))
        lam = _lambda_full(lq1_ref[...], lk1_ref[...], lq2_ref[...], lk2_ref[...], lam_init)
        for h in range(HEADS):
            r1, r2 = h * hrows, h * hrows + QROWS
            oh = _head_out(acc_fin[r1:r1 + QROWS], l_fin[r1:r1 + QROWS],
                           acc_fin[r2:r2 + QROWS], l_fin[r2:r2 + QROWS],
                           lam, sg_ref[...], lam_init)
            o_ref[0, :, h * HEAD_W:(h + 1) * HEAD_W] = oh[0:dec]


def _attn_sample(q, k_new, v_new, cache_k, cache_v, layer, page_table, rb_flat,
                 lq1, lk1, lq2, lk2, subln_g, dec, lam_init, pp):
    nseq, n_pages = page_table.shape
    depth, n_phys, page = cache_k.shape[:3]
    assert dec <= QROWS and page >= _FAR_DIST and n_pages % pp == 0
    n_steps = n_pages // pp
    q3 = q.reshape(nseq, dec, ATT_W)
    kn3 = k_new.reshape(nseq, dec * HEADS, HEAD_W)
    vn3 = v_new.reshape(nseq, dec * HEADS, HEAD_W)
    new_spec = pl.BlockSpec((1, dec * HEADS, HEAD_W), lambda b, p, pt: (b, 0, 0))
    prow = page * HEADS
    kc = cache_k.reshape(depth * n_phys * prow, HEAD_W)
    vc = cache_v.reshape(depth * n_phys * prow, HEAD_W)
    rows = 2 * HEADS * QROWS
    seq_spec = pl.BlockSpec((1, dec, ATT_W), lambda b, p, pt: (b, 0, 0))

    def page_spec(r):
        return pl.BlockSpec((prow, HEAD_W),
                            lambda b, p, pt: (layer * n_phys + pt[b, p * pp + r], 0))

    pages = [page_spec(r) for r in range(pp)]
    small = lambda n: pl.BlockSpec((1, n), lambda b, p, pt: (0, 0))
    grid_spec = pltpu.PrefetchScalarGridSpec(
        num_scalar_prefetch=1,
        grid=(nseq, n_steps),
        in_specs=[pl.BlockSpec(memory_space=pltpu.SMEM), seq_spec, new_spec, new_spec]
        + pages + pages
        + [small(QK_DIM), small(QK_DIM), small(QK_DIM), small(QK_DIM), small(HEAD_W)],
        out_specs=seq_spec,
        scratch_shapes=[pltpu.VMEM((QROWS, ATT_W), F32),
                        pltpu.VMEM((rows, HEAD_W), F32),
                        pltpu.VMEM((n_steps, rows, pp * prow), F32),
                        pltpu.VMEM((rows, 1), F32),
                        pltpu.VMEM((rows, 1), F32),
                        pltpu.VMEM((rows, HEAD_W), F32)])
    out = pl.pallas_call(
        functools.partial(_attn_sample_kernel, n_steps=n_steps, pp=pp, page=page, dec=dec,
                          lam_init=lam_init),
        grid_spec=grid_spec,
        out_shape=jax.ShapeDtypeStruct((nseq, dec, ATT_W), F32),
        compiler_params=_cparams(("arbitrary", "arbitrary")),
        name="attn_sample",
    )(page_table, rb_flat, q3, kn3, vn3, *([kc] * pp), *([vc] * pp), lq1, lk1, lq2, lk2, subln_g)
    return out.reshape(nseq * dec, ATT_W)


def _pool_groups(window_sum, u_of, cnt_of, pw_ref, ps_ref):
    outs = []
    for g, w in enumerate(POOL_WINDOWS):
        d = window_sum(g, w) / cnt_of(w) - u_of(g)
        outs.append(jnp.dot(d.astype(BF16), pw_ref[g], preferred_element_type=F32))
    return jnp.concatenate(outs, axis=-1) * ps_ref[...]


def _mix_prompt_kernel(x_ref, a_ref, u_ref, halo_ref, pw_ref, ps_ref, wo_ref, o_ref, z_ref,
                       *, tm, tiles_per_seq):
    i = pl.program_id(0)
    t_in_seq = i % tiles_per_seq
    z_ref[0:POOL_HALO, :] = jnp.where(t_in_seq == 0, 0.0, halo_ref[...])
    z_ref[POOL_HALO:POOL_HALO + tm, :] = u_ref[...]
    pos = t_in_seq * tm + lax.broadcasted_iota(jnp.int32, (tm, 1), 0)

    def window_sum(g, w):
        cs = slice(g * POOL_GW, (g + 1) * POOL_GW)
        acc = z_ref[POOL_HALO:POOL_HALO + tm, cs]
        for k in range(1, w):
            acc = acc + z_ref[POOL_HALO - k:POOL_HALO - k + tm, cs]
        return acc

    pool = _pool_groups(
        window_sum,
        lambda g: u_ref[:, g * POOL_GW:(g + 1) * POOL_GW],
        lambda w: jnp.minimum(pos + 1, w).astype(F32),
        pw_ref, ps_ref)
    mixed = jnp.concatenate([a_ref[...], pool], axis=-1).astype(BF16)
    o_ref[...] = x_ref[...] + jnp.dot(mixed, wo_ref[...], preferred_element_type=F32)


def _mix_prompt(x, attn, u, pool_w_bf, pool_scale, w_out_bf, seq, tm):
    t, d = x.shape
    pw_cols = u.shape[1]
    assert seq % tm == 0 and tm % POOL_HALO == 0
    halo_blocks = tm // POOL_HALO
    return pl.pallas_call(
        functools.partial(_mix_prompt_kernel, tm=tm, tiles_per_seq=seq // tm),
        grid=(t // tm,),
        in_specs=[pl.BlockSpec((tm, d), lambda i: (i, 0)),
                  pl.BlockSpec((tm, ATT_W), lambda i: (i, 0)),
                  pl.BlockSpec((tm, pw_cols), lambda i: (i, 0)),
                  pl.BlockSpec((POOL_HALO, pw_cols),
                               lambda i: (jnp.maximum(i * halo_blocks - 1, 0), 0)),
                  pl.BlockSpec(pool_w_bf.shape, lambda i: (0, 0, 0)),
                  pl.BlockSpec((1, pw_cols), lambda i: (0, 0)),
                  pl.BlockSpec(w_out_bf.shape, lambda i: (0, 0))],
        out_specs=pl.BlockSpec((tm, d), lambda i: (i, 0)),
        out_shape=jax.ShapeDtypeStruct((t, d), F32),
        scratch_shapes=[pltpu.VMEM((POOL_HALO + tm, pw_cols), F32)],
        compiler_params=_cparams(("arbitrary",)),
        name="mix_prompt",
    )(x, attn, u, u, pool_w_bf, pool_scale, w_out_bf)


def _mix_sample_kernel(x_ref, a_ref, u_ref, st_ref, pw_ref, ps_ref, wo_ref, o_ref,
                       *, dec, start_pos):
    def z_row(r):
        return st_ref[r] if r < POOL_PAD else u_ref[r - POOL_PAD]

    for i in range(dec):
        def window_sum(g, w, i=i):
            cs = slice(g * POOL_GW, (g + 1) * POOL_GW)
            acc = z_row(POOL_PAD + i)[:, cs]
            for k in range(1, w):
                acc = acc + z_row(POOL_PAD + i - k)[:, cs]
            return acc

        pool = _pool_groups(
            window_sum,
            lambda g, i=i: u_ref[i][:, g * POOL_GW:(g + 1) * POOL_GW],
            lambda w, i=i: float(min(start_pos + i + 1, w)),
            pw_ref, ps_ref)
        mixed = jnp.concatenate([a_ref[i], pool], axis=-1).astype(BF16)
        o_ref[i] = x_ref[i] + jnp.dot(mixed, wo_ref[...], preferred_element_type=F32)


def _mix_sample(x3, attn3, u3, state3, pool_w_bf, pool_scale, w_out_bf, start_pos):
    dec, nseq, d = x3.shape
    full = lambda a: pl.BlockSpec(a.shape, lambda i: (0,) * a.ndim)
    args = (x3, attn3, u3, state3, pool_w_bf, pool_scale, w_out_bf)
    return pl.pallas_call(
        functools.partial(_mix_sample_kernel, dec=dec, start_pos=start_pos),
        grid=(1,),
        in_specs=[full(a) for a in args],
        out_specs=full(x3),
        out_shape=jax.ShapeDtypeStruct((dec, nseq, d), F32),
        compiler_params=_cparams(("arbitrary",)),
        name="mix_sample",
    )(*args)


def _erf_gelu(x):
    return 0.5 * x * (1.0 + lax.erf(x * (2.0 ** -0.5)))


def _sort_network(n):
    pairs = []

    def merge(lo, cnt, r):
        step = 2 * r
        if step < cnt:
            merge(lo, cnt, step)
            merge(lo + r, cnt, step)
            pairs.extend((i, i + r) for i in range(lo + r, lo + cnt - r, step))
        else:
            pairs.append((lo, lo + r))

    def sort(lo, cnt):
        if cnt > 1:
            sort(lo, cnt // 2)
            sort(lo + cnt // 2, cnt // 2)
            merge(lo, cnt, 1)

    sort(0, n)
    return pairs


def _top_rows(x, n):
    sub = 8
    v = [x[i * sub:(i + 1) * sub] for i in range(x.shape[0] // sub)]
    size = 1 << (len(v) - 1).bit_length()
    v += [None] * (size - len(v))
    for i, j in _sort_network(size):
        if v[j] is None:
            continue
        if v[i] is None:
            v[i], v[j] = v[j], None
        else:
            v[i], v[j] = jnp.maximum(v[i], v[j]), jnp.minimum(v[i], v[j])
    v = [t for t in v if t is not None]
    rows = []
    for r in range(n):
        m = jnp.max(v[0], axis=0, keepdims=True)
        rows.append(m)
        left = n - r - 1
        if left:
            hit = v[0] == m
            for k in range(min(left, len(v))):
                nxt = v[k + 1] if k + 1 < len(v) else -jnp.inf
                v[k] = jnp.where(hit, nxt, v[k])
    return rows


def _rank_among(x, rows):
    rank = jnp.zeros(x.shape, F32)
    for q, row in enumerate(rows):
        rank = jnp.where(row > x, float(q + 1), rank)
    return rank


def _stack_rows(rows):
    n = len(rows)
    rid = lax.broadcasted_iota(jnp.int32, (n, LANES), 0)
    out = jnp.broadcast_to(rows[0], (n, LANES))
    for r in range(1, n):
        out = jnp.where(rid == r, rows[r], out)
    return out


def _peer_route_block(s1, s2):
    k = PEER_TOPK
    t1 = _top_rows(s1, k)
    t2 = _top_rows(s2, k)
    rank2 = _rank_among(s2, t2)
    t1s = _stack_rows(t1)
    t2s = _stack_rows(t2)
    half = k // 2
    rid = lax.broadcasted_iota(jnp.int32, (half, LANES), 0)
    cands = [t1s[0:half] + t2[0], t1s[half:k] + t2[0], t1s[0:half] + t2[1]]
    for q in range(2, half):
        cands.append(jnp.where(rid < k // (q + 1), t1s[0:half] + t2[q], -jnp.inf))
    cands.append(t2s[half:k] + t1[0])
    cand = jnp.concatenate(cands, axis=0)
    tau = _top_rows(cand, k)[k - 1]
    top = t1[0] + t2[0]
    z = jnp.sum(jnp.where(cand >= tau, jnp.exp(cand - top), 0.0), axis=0, keepdims=True)
    cnt = jnp.zeros(s1.shape, F32)
    for q in range(k):
        cnt = jnp.where(s1 + t2[q] >= tau, float(q + 1), cnt)
    a = jnp.exp(s1 - t1[0])
    b = jnp.exp(s2 - t2[0]) / z
    return cnt, rank2, a, b


def _peer_kernel(x_ref, g_ref, wq_ref, keys_ref, u_ref, vt_ref, o_ref,
                 xn_ref, s1_ref, s2_ref, ca_ref, rb_ref, w_ref, acc_ref,
                 *, tt, ec):
    j = pl.program_id(1)
    n_chunks = pl.num_programs(1)
    nsb = tt // LANES
    c_per = ec // PEER_KEYS
    pk = BF16_SUBLANES

    n_pieces = nsb * PEER_KEYS // GATE_ROWS

    def gates(chunk, w_ref, pieces=range(n_pieces)):
        zero = jnp.zeros((pk, LANES), BF16)
        c0 = pl.multiple_of(chunk * c_per, c_per)
        gpp = GATE_ROWS // pk
        for piece in pieces:
            sb, r0 = piece // (PEER_KEYS // GATE_ROWS), (piece % (PEER_KEYS // GATE_ROWS)) * GATE_ROWS
            w = [[zero] * gpp for _ in range(c_per)]
            for h in range(PEER_HEADS):
                cnt = [jnp.broadcast_to(ca_ref[h, sb, pl.ds(c0 + cl, 1), :], (pk, LANES)).astype(BF16)
                       for cl in range(c_per)]
                a = [jnp.broadcast_to(ca_ref[h, sb, pl.ds(A_OFF + c0 + cl, 1), :],
                                      (pk, LANES)).astype(BF16) for cl in range(c_per)]
                for g in range(gpp):
                    lo = r0 + g * pk
                    rank = rb_ref[h, sb, lo:lo + pk, :]
                    b = rb_ref[h, sb, B_OFF + lo:B_OFF + lo + pk, :]
                    for cl in range(c_per):
                        w[cl][g] = w[cl][g] + jnp.where(rank < cnt[cl], b * a[cl], zero)
            for cl in range(c_per):
                for g in range(gpp):
                    e0 = cl * PEER_KEYS + r0 + g * pk
                    w_ref[e0:e0 + pk, sb * LANES:(sb + 1) * LANES] = w[cl][g]

    @pl.when(j == 0)
    def _route():
        xn = _rmsnorm_rows(x_ref[...], g_ref[...]).astype(BF16)
        xn_ref[...] = xn

        def head_scores(h, carry):
            qh = jnp.dot(xn_ref[...], wq_ref[h], preferred_element_type=F32)
            half = qh.shape[1] // 2
            for part, dst in ((0, s1_ref), (1, s2_ref)):
                qp = qh[:, part * half:(part + 1) * half].astype(BF16)
                st = lax.dot_general(keys_ref[h, part], qp, (((1,), (1,)), ((), ())),
                                     preferred_element_type=F32)
                for sb in range(nsb):
                    dst[h, sb] = st[:, sb * LANES:(sb + 1) * LANES]
            return carry

        lax.fori_loop(0, PEER_HEADS, head_scores, 0)

        def route(idx, carry):
            h, sb = idx // nsb, idx % nsb
            cnt, rank2, a, b = _peer_route_block(s1_ref[h, sb], s2_ref[h, sb])
            ca_ref[h, sb, 0:PEER_KEYS, :] = cnt
            ca_ref[h, sb, A_OFF:A_OFF + PEER_KEYS, :] = a
            rb_ref[h, sb, 0:PEER_KEYS, :] = rank2.astype(BF16)
            rb_ref[h, sb, B_OFF:B_OFF + PEER_KEYS, :] = b.astype(BF16)
            return carry

        lax.fori_loop(0, PEER_HEADS * nsb, route, 0)
        acc_ref[...] = jnp.zeros(acc_ref.shape, F32)

    gates(j, w_ref)
    at = lax.dot_general(u_ref[...], xn_ref[...], (((1,), (1,)), ((), ())),
                         preferred_element_type=F32)
    p = w_ref[...] * _erf_gelu(at).astype(BF16)
    acc_ref[...] += jnp.dot(vt_ref[...], p, preferred_element_type=F32)

    @pl.when(j == n_chunks - 1)
    def _finish():
        o_ref[...] = x_ref[...] + acc_ref[...].T


def _peer(x, g, wq_heads_bf, keys_bf, u_bf, vt_bf, tt, ec):
    t, d = x.shape
    n_exp = u_bf.shape[0]
    assert t % tt == 0 and n_exp % ec == 0 and tt % LANES == 0 and ec % PEER_KEYS == 0
    assert n_exp == PEER_KEYS * PEER_KEYS
    nsb = tt // LANES
    blk = lambda dt: pltpu.VMEM((PEER_HEADS, nsb, PEER_KEYS, LANES), dt)
    return pl.pallas_call(
        functools.partial(_peer_kernel, tt=tt, ec=ec),
        grid=(t // tt, n_exp // ec),
        in_specs=[pl.BlockSpec((tt, d), lambda i, j: (i, 0)),
                  pl.BlockSpec((1, d), lambda i, j: (0, 0)),
                  pl.BlockSpec(wq_heads_bf.shape, lambda i, j: (0, 0, 0)),
                  pl.BlockSpec(keys_bf.shape, lambda i, j: (0, 0, 0, 0)),
                  pl.BlockSpec((ec, d), lambda i, j: (j, 0)),
                  pl.BlockSpec((d, ec), lambda i, j: (0, j))],
        out_specs=pl.BlockSpec((tt, d), lambda i, j: (i, 0)),
        out_shape=jax.ShapeDtypeStruct((t, d), F32),
        scratch_shapes=[pltpu.VMEM((tt, d), BF16),
                        blk(F32), blk(F32),
                        pltpu.VMEM((PEER_HEADS, nsb, A_OFF + PEER_KEYS, LANES), F32),
                        pltpu.VMEM((PEER_HEADS, nsb, B_OFF + PEER_KEYS, LANES), BF16),
                        pltpu.VMEM((ec, tt), BF16),
                        pltpu.VMEM((d, tt), F32)],
        compiler_params=_cparams(("arbitrary", "arbitrary")),
        name="peer",
    )(x, g, wq_heads_bf, keys_bf, u_bf, vt_bf)


def _tail_kernel(x_ref, p_ref, gpl_ref, wg_ref, wpl_ref, gf_ref, o_ref):
    x = x_ref[...]
    xn = _rmsnorm_rows(x, gpl_ref[...]).astype(BF16)
    gate = jax.nn.sigmoid(jnp.dot(xn, wg_ref[...], preferred_element_type=F32))
    emb = jnp.dot(p_ref[...].astype(BF16), wpl_ref[...], preferred_element_type=F32)
    o_ref[...] = _rmsnorm_rows(x + gate * emb, gf_ref[...])


def _tail(x, p, g_pl, w_gate_bf, w_pl_bf, g_final, tm):
    t, d = x.shape
    pd = p.shape[1]
    return pl.pallas_call(
        _tail_kernel,
        grid=(t // tm,),
        in_specs=[pl.BlockSpec((tm, d), lambda i: (i, 0)),
                  pl.BlockSpec((tm, pd), lambda i: (i, 0)),
                  pl.BlockSpec((1, d), lambda i: (0, 0)),
                  pl.BlockSpec((d, d), lambda i: (0, 0)),
                  pl.BlockSpec((pd, d), lambda i: (0, 0)),
                  pl.BlockSpec((1, d), lambda i: (0, 0))],
        out_specs=pl.BlockSpec((tm, d), lambda i: (i, 0)),
        out_shape=jax.ShapeDtypeStruct((t, d), F32),
        compiler_params=_cparams(("arbitrary",)),
        name="tail",
    )(x, p, g_pl, w_gate_bf, w_pl_bf, g_final)


def _row_tile(t, want):
    tm = min(t, want)
    assert t % tm == 0
    return tm


def kernel(x_prompt, x_sample, cache_k, cache_v, page_table, state_pool, p_prompt, p_sample, norm_attn_g, w_in, lambda_q1, lambda_k1, lambda_q2, lambda_k2, subln_g, rel_bias, pool_w, pool_scale, w_out, norm_ffn_g, peer_wq, peer_keys, peer_u, peer_v, norm_pl_g, w_pl, w_pl_gate, final_norm_g):
    depth = w_in.shape[0]
    assert depth == 1
    batch, seq, d = x_prompt.shape
    nseq, dec, _ = x_sample.shape
    n_pages = page_table.shape[1]
    page = cache_k.shape[2]
    past_len = n_pages * page
    lam_init = 0.8 - 0.6 * math.exp(-0.3 * 0)
    i = 0

    row = lambda a: a.reshape(1, -1)
    w_in_bf = w_in[i].astype(BF16)
    pool_w_bf = pool_w[i].astype(BF16)
    w_out_bf = w_out[i].astype(BF16)
    qd = peer_wq.shape[2] // PEER_HEADS
    wq_heads_bf = peer_wq[i].reshape(d, PEER_HEADS, qd).transpose(1, 0, 2).astype(BF16)
    keys_bf = peer_keys[i].astype(BF16)
    u_bf = peer_u[i].astype(BF16)
    vt_bf = peer_v[i].T.astype(BF16)
    w_gate_bf = w_pl_gate[i].astype(BF16)
    w_pl_bf = w_pl[i].astype(BF16)
    rb_flat = rel_bias.reshape(-1)
    lam_rows = (row(lambda_q1[i]), row(lambda_k1[i]), row(lambda_q2[i]), row(lambda_k2[i]))
    sg = row(subln_g[i])

    tp = batch * seq
    xp = x_prompt.reshape(tp, d)
    _, k, v, u, qb, kb, vtb = _inproj(xp, row(norm_attn_g[i]), w_in_bf, _row_tile(tp, 512))
    attn = _attn_prompt(qb, kb, vtb, rb_flat, *lam_rows, subln_g[i].reshape(-1, 1),
                        batch, seq, min(seq, 512), lam_init)
    x1 = _mix_prompt(xp, attn, u, pool_w_bf, row(pool_scale[i]), w_out_bf, seq, min(seq, 512))
    x2 = _peer(x1, row(norm_ffn_g[i]), wq_heads_bf, keys_bf, u_bf, vt_bf,
               _row_tile(tp, 512), min(u_bf.shape[0], 1024))
    y_prompt = _tail(x2, p_prompt[i].reshape(tp, -1), row(norm_pl_g[i]), w_gate_bf, w_pl_bf,
                     row(final_norm_g), _row_tile(tp, 512)).reshape(batch, seq, d)
    new_k_prompt = k.reshape(1, batch, seq, HEADS, HEAD_W)
    new_v_prompt = v.reshape(1, batch, seq, HEADS, HEAD_W)
    new_pool_prompt = u.reshape(batch, seq, -1)[:, seq - POOL_PAD:, :][None]

    ts = nseq * dec
    xs = x_sample.reshape(ts, d)
    qs, ks, vs, us, _, _, _ = _inproj(xs, row(norm_attn_g[i]), w_in_bf, _row_tile(ts, 512))
    attn_s = _attn_sample(qs, ks, vs, cache_k, cache_v, i, page_table, rb_flat,
                          *lam_rows, sg, dec, lam_init, math.gcd(n_pages, 16))
    tok_major = lambda a: a.reshape(nseq, dec, -1).transpose(1, 0, 2)
    x1s = _mix_sample(tok_major(xs), tok_major(attn_s), tok_major(us),
                      state_pool[i].transpose(1, 0, 2), pool_w_bf, row(pool_scale[i]),
                      w_out_bf, past_len)
    x1s = x1s.transpose(1, 0, 2).reshape(ts, d)
    x2s = _peer(x1s, row(norm_ffn_g[i]), wq_heads_bf, keys_bf, u_bf, vt_bf,
                _row_tile(ts, 512), min(u_bf.shape[0], 1024))
    y_sample = _tail(x2s, p_sample[i].reshape(ts, -1), row(norm_pl_g[i]), w_gate_bf, w_pl_bf,
                     row(final_norm_g), _row_tile(ts, 512)).reshape(nseq, dec, d)
    new_k_sample = ks.reshape(1, nseq, dec, HEADS, HEAD_W)
    new_v_sample = vs.reshape(1, nseq, dec, HEADS, HEAD_W)
    new_pool_sample = jnp.concatenate(
        [state_pool[i][:, dec:, :], us.reshape(nseq, dec, -1)], axis=1)[None]

    return (y_prompt, y_sample, new_k_prompt, new_v_prompt, new_pool_prompt,
            new_k_sample, new_v_sample, new_pool_sample)
```

```python
import functools
import math

import jax
import jax.numpy as jnp
import numpy as np
from jax import lax
from jax.experimental import pallas as pl
from jax.experimental.pallas import tpu as pltpu

F32 = jnp.float32
BF16 = jnp.bfloat16

EPS = 1e-6
NEG_INF = -1e30
LOG2E = 1.4426950408889634
HEADS = 4
QK_DIM = 64
HEAD_W = 2 * QK_DIM
ATT_W = HEADS * HEAD_W
POOL_WINDOWS = (2, 4, 8, 16)
POOL_GW = 128
POOL_PAD = 15
POOL_HALO = 16
REL_BUCKETS = 32
REL_MAX_DIST = 128
PEER_HEADS = 8
PEER_KEYS = 128
PEER_TOPK = 16
LANES = 128
BF16_SUBLANES = 16
GATE_ROWS = 32
A_OFF = PEER_KEYS + 8
B_OFF = PEER_KEYS + BF16_SUBLANES
ONES_ROWS = 16
VMEM_LIMIT = 56 * 1024 * 1024


def _bucket_thresholds():
    exact = REL_BUCKETS // 2
    n = np.arange(0, 4 * REL_MAX_DIST)
    large = exact + (np.log(np.maximum(n, 1).astype(np.float32) / np.float32(exact))
                     / np.float32(math.log(REL_MAX_DIST / exact))
                     * np.float32(REL_BUCKETS - exact)).astype(np.int32)
    large = np.minimum(large, REL_BUCKETS - 1)
    bucket = np.where(n < exact, n, large)
    return [int(n[np.argmax(bucket >= b)]) for b in range(REL_BUCKETS)]


_BUCKET_START = _bucket_thresholds()
_FAR_DIST = _BUCKET_START[REL_BUCKETS - 1]


def _rel_bias_tile(rel, rb_ref, h):
    out = jnp.full(rel.shape, rb_ref[h], F32)
    for b in range(1, REL_BUCKETS):
        out = jnp.where(rel >= _BUCKET_START[b], rb_ref[b * HEADS + h], out)
    return out


def _rmsnorm_rows(x, g):
    return x * lax.rsqrt(jnp.mean(x * x, axis=-1, keepdims=True) + EPS) * g


def _cparams(sem):
    return pltpu.CompilerParams(dimension_semantics=sem, vmem_limit_bytes=VMEM_LIMIT)


def _inproj_kernel(x_ref, g_ref, w_ref, q_ref, k_ref, v_ref, u_ref, qb_ref, kb_ref, vtb_ref):
    xn = _rmsnorm_rows(x_ref[...], g_ref[...])
    proj = jnp.dot(xn.astype(BF16), w_ref[...], preferred_element_type=F32)
    q = proj[:, 0 * ATT_W:1 * ATT_W]
    k = proj[:, 1 * ATT_W:2 * ATT_W]
    v = proj[:, 2 * ATT_W:3 * ATT_W]
    q_ref[...] = q
    tm = q.shape[0]
    for h in range(HEADS):
        k_ref[pl.ds(h, tm, stride=HEADS), :] = k[:, h * HEAD_W:(h + 1) * HEAD_W]
        v_ref[pl.ds(h, tm, stride=HEADS), :] = v[:, h * HEAD_W:(h + 1) * HEAD_W]
    u_ref[...] = proj[:, 3 * ATT_W:4 * ATT_W]
    qb_ref[...] = (q * (QK_DIM ** -0.5 * LOG2E)).astype(BF16)
    kb_ref[...] = k.astype(BF16)
    vtb_ref[...] = v.T.astype(BF16)


def _inproj(x, g, w_bf, tm):
    t, d = x.shape
    n_out = w_bf.shape[1]
    assert n_out == 4 * ATT_W and t % tm == 0
    out = jax.ShapeDtypeStruct((t, ATT_W), F32)
    out_kv = jax.ShapeDtypeStruct((t * HEADS, HEAD_W), F32)
    out_bf = jax.ShapeDtypeStruct((t, ATT_W), BF16)
    row = pl.BlockSpec((tm, ATT_W), lambda i: (i, 0))
    row_kv = pl.BlockSpec((tm * HEADS, HEAD_W), lambda i: (i, 0))
    return pl.pallas_call(
        _inproj_kernel,
        grid=(t // tm,),
        in_specs=[pl.BlockSpec((tm, d), lambda i: (i, 0)),
                  pl.BlockSpec((1, d), lambda i: (0, 0)),
                  pl.BlockSpec((d, n_out), lambda i: (0, 0))],
        out_specs=[row, row_kv, row_kv, row, row, row, pl.BlockSpec((ATT_W, tm), lambda i: (0, i))],
        out_shape=[out, out_kv, out_kv, out, out_bf, out_bf,
                   jax.ShapeDtypeStruct((ATT_W, t), BF16)],
        compiler_params=_cparams(("arbitrary",)),
        name="inproj",
    )(x, g, w_bf)


def _lambda_full(lq1, lk1, lq2, lk2, lam_init):
    return (jnp.exp(jnp.sum(lq1 * lk1, axis=-1, keepdims=True))
            - jnp.exp(jnp.sum(lq2 * lk2, axis=-1, keepdims=True)) + lam_init)


def _head_out(acc1, l1, acc2, l2, lam, subln_g, lam_init):
    o = acc1 / l1 - lam * (acc2 / l2)
    return _rmsnorm_rows(o, subln_g) * (1.0 - lam_init)


def _attn_prompt_kernel(rb_ref, q_ref, k_ref, vt_ref, lq1_ref, lk1_ref, lq2_ref, lk2_ref, sg_ref,
                        o_ref, bias_ref, ve_ref, m_ref, acc_ref, *, tb, lam_init):
    b, qi, kj = pl.program_id(0), pl.program_id(1), pl.program_id(2)

    @pl.when((b == 0) & (qi == 0) & (kj == 0))
    def _build_constants():
        key = lax.broadcasted_iota(jnp.int32, (tb, tb), 0)
        qry = lax.broadcasted_iota(jnp.int32, (tb, tb), 1)
        rel = qry - key
        for h in range(HEADS):
            diag = _rel_bias_tile(rel, rb_ref, h) * LOG2E
            bias_ref[0, h] = jnp.where(rel >= 0, diag, NEG_INF)
            bias_ref[1, h] = _rel_bias_tile(rel + tb, rb_ref, h) * LOG2E
            ve_ref[h, HEAD_W:HEAD_W + ONES_ROWS, :] = jnp.ones((ONES_ROWS, tb), ve_ref.dtype)

    @pl.when(kj == 0)
    def _init():
        m_ref[...] = jnp.full(m_ref.shape, -jnp.inf, F32)
        acc_ref[...] = jnp.zeros(acc_ref.shape, F32)

    def step(near):
        for h in range(HEADS):
            ve_ref[h, 0:HEAD_W, :] = vt_ref[h * HEAD_W:(h + 1) * HEAD_W, :]
        for h in range(HEADS):
            for mp in range(2):
                c0 = h * HEAD_W + mp * QK_DIM
                idx = h * 2 + mp
                st = lax.dot_general(k_ref[:, c0:c0 + QK_DIM], q_ref[:, c0:c0 + QK_DIM],
                                     (((1,), (1,)), ((), ())), preferred_element_type=F32)
                m_old = m_ref[idx]
                if near:
                    st = st + bias_ref[jnp.where(kj == qi, 0, 1), h]
                    m_new = jnp.maximum(m_old, jnp.max(st, axis=0, keepdims=True))
                    shift = m_new
                else:
                    far = rb_ref[(REL_BUCKETS - 1) * HEADS + h] * LOG2E
                    m_new = jnp.maximum(m_old, jnp.max(st, axis=0, keepdims=True) + far)
                    shift = m_new - far
                alpha = jnp.exp2(m_old - m_new)
                p = jnp.exp2(st - shift).astype(BF16)
                acc_ref[idx] = alpha * acc_ref[idx] + jnp.dot(
                    ve_ref[h], p, preferred_element_type=F32)
                m_ref[idx] = m_new

    @pl.when((kj <= qi) & (kj >= qi - 1))
    def _near_step():
        step(True)

    @pl.when(kj < qi - 1)
    def _far_step():
        step(False)

    @pl.when(kj == qi)
    def _finish():
        lam = _lambda_full(lq1_ref[...], lk1_ref[...], lq2_ref[...], lk2_ref[...], lam_init)
        for h in range(HEADS):
            a1, a2 = acc_ref[2 * h], acc_ref[2 * h + 1]
            o = (a1[0:HEAD_W] / a1[HEAD_W:HEAD_W + 1]
                 - lam * (a2[0:HEAD_W] / a2[HEAD_W:HEAD_W + 1]))
            ms = jnp.mean(o * o, axis=0, keepdims=True)
            on = o * lax.rsqrt(ms + EPS) * sg_ref[...] * (1.0 - lam_init)
            o_ref[:, h * HEAD_W:(h + 1) * HEAD_W] = on.T


def _attn_prompt(q, k, vt, rb_flat, lq1, lk1, lq2, lk2, subln_col, batch, seq, tb, lam_init):
    t = q.shape[0]
    assert t == batch * seq and seq % tb == 0 and tb >= _FAR_DIST and q.dtype == BF16
    nq = seq // tb
    small = lambda n: pl.BlockSpec((1, n), lambda b, i, j: (0, 0))
    k_spec = pl.BlockSpec((tb, ATT_W), lambda b, i, j: (b * nq + jnp.minimum(j, i), 0))
    vt_spec = pl.BlockSpec((ATT_W, tb), lambda b, i, j: (0, b * nq + jnp.minimum(j, i)))
    q_spec = pl.BlockSpec((tb, ATT_W), lambda b, i, j: (b * nq + i, 0))
    return pl.pallas_call(
        functools.partial(_attn_prompt_kernel, tb=tb, lam_init=lam_init),
        grid=(batch, nq, nq),
        in_specs=[pl.BlockSpec(memory_space=pltpu.SMEM),
                  q_spec, k_spec, vt_spec,
                  small(QK_DIM), small(QK_DIM), small(QK_DIM), small(QK_DIM),
                  pl.BlockSpec((HEAD_W, 1), lambda b, i, j: (0, 0))],
        out_specs=q_spec,
        out_shape=jax.ShapeDtypeStruct((t, ATT_W), F32),
        scratch_shapes=[pltpu.VMEM((2, HEADS, tb, tb), F32),
                        pltpu.VMEM((HEADS, HEAD_W + ONES_ROWS, tb), BF16),
                        pltpu.VMEM((2 * HEADS, 1, tb), F32),
                        pltpu.VMEM((2 * HEADS, HEAD_W + ONES_ROWS, tb), F32)],
        compiler_params=_cparams(("arbitrary", "arbitrary", "arbitrary")),
        name="attn_prompt",
    )(rb_flat, q, k, vt, lq1, lk1, lq2, lk2, subln_col)


QROWS = 8


def _attn_sample_kernel(pt_ref, rb_ref, q_ref, kn_ref, vn_ref, *rest,
                        n_steps, pp, page, dec, lam_init):
    del pt_ref
    k_refs, v_refs = rest[:pp], rest[pp:2 * pp]
    (lq1_ref, lk1_ref, lq2_ref, lk2_ref, sg_ref, o_ref,
     q8_ref, qcat_ref, bias_ref, m_ref, l_ref, acc_ref) = rest[2 * pp:]
    b, p = pl.program_id(0), pl.program_id(1)
    hrows = 2 * QROWS
    rows = HEADS * hrows
    prow = page * HEADS
    width = pp * prow
    past_len = n_steps * pp * page
    row_i = lax.broadcasted_iota(jnp.int32, (rows, 1), 0) % QROWS

    def per_head(fn):
        return jnp.concatenate([fn(h) for h in range(HEADS)], axis=0)

    @pl.when(b == 0)
    def _build_bias():
        col = lax.broadcasted_iota(jnp.int32, (hrows, width), 1)
        k_pos = p * (pp * page) + col // HEADS
        rel = past_len + lax.broadcasted_iota(jnp.int32, (hrows, width), 0) % QROWS - k_pos
        bias_ref[p] = per_head(lambda h: jnp.where(col % HEADS == h,
                                                   _rel_bias_tile(rel, rb_ref, h), NEG_INF))

    @pl.when(p == 0)
    def _init():
        q8_ref[...] = jnp.zeros(q8_ref.shape, F32)
        q8_ref[0:dec, :] = q_ref[0] * (QK_DIM ** -0.5)
        col = lax.broadcasted_iota(jnp.int32, (QROWS, HEAD_W), 1)
        for h in range(HEADS):
            qh = q8_ref[:, h * HEAD_W:(h + 1) * HEAD_W]
            for mp in range(2):
                r0 = h * hrows + mp * QROWS
                qcat_ref[r0:r0 + QROWS, :] = jnp.where(col // QK_DIM == mp, qh, 0.0)
        m_ref[...] = jnp.full(m_ref.shape, -jnp.inf, F32)
        l_ref[...] = jnp.zeros(l_ref.shape, F32)
        acc_ref[...] = jnp.zeros(acc_ref.shape, F32)

    qcat = qcat_ref[...]
    qcat_bf = qcat.astype(BF16)
    s = jnp.concatenate(
        [lax.dot_general(qcat_bf, k_refs[r][...].astype(BF16), (((1,), (1,)), ((), ())),
                         preferred_element_type=F32) for r in range(pp)], axis=1)
    s = s + bias_ref[p]
    m_old = m_ref[...]
    m_new = jnp.maximum(m_old, jnp.max(s, axis=-1, keepdims=True))
    alpha = jnp.exp(m_old - m_new)
    pr = jnp.exp(s - m_new)
    l_ref[...] = alpha * l_ref[...] + jnp.sum(pr, axis=-1, keepdims=True)
    pv = jnp.dot(pr[:, 0:prow].astype(BF16), v_refs[0][...].astype(BF16),
                 preferred_element_type=F32)
    for r in range(1, pp):
        pv = pv + jnp.dot(pr[:, r * prow:(r + 1) * prow].astype(BF16),
                          v_refs[r][...].astype(BF16), preferred_element_type=F32)
    acc_ref[...] = alpha * acc_ref[...] + pv
    m_ref[...] = m_new

    @pl.when(p == n_steps - 1)
    def _finish():
        def head_row(ref, j, h):
            return ref[0, j * HEADS + h:j * HEADS + h + 1, :]

        s_new = []
        for j in range(dec):
            sj = per_head(lambda h, j=j: jnp.sum(
                qcat[h * hrows:(h + 1) * hrows] * head_row(kn_ref, j, h), axis=-1, keepdims=True)
                + _rel_bias_tile(row_i[0:hrows] - j, rb_ref, h))
            s_new.append(jnp.where(row_i >= j, sj, NEG_INF))
        m_old2 = m_ref[...]
        m_fin = m_old2
        for sj in s_new:
            m_fin = jnp.maximum(m_fin, sj)
        alpha2 = jnp.exp(m_old2 - m_fin)
        l_fin = alpha2 * l_ref[...]
        acc_fin = alpha2 * acc_ref[...]
        for j, sj in enumerate(s_new):
            pj = jnp.exp(sj - m_fin)
            l_fin = l_fin + pj
            acc_fin = acc_fin + pj * per_head(
                lambda h, j=j: jnp.broadcast_to(head_row(vn_ref, j, h), (hrows, HEAD_W)))
        lam = _lambda_full(lq1_ref[...], lk1_ref[...], lq2_ref[...], lk2_ref[...], lam_init)
        for h in range(HEADS):
            r1, r2 = h * hrows, h * hrows + QROWS
            oh = _head_out(acc_fin[r1:r1 + QROWS], l_fin[r1:r1 + QROWS],
                           acc_fin[r2:r2 + QROWS], l_fin[r2:r2 + QROWS],
                           lam, sg_ref[...], lam_init)
            o_ref[0, :, h * HEAD_W:(h + 1) * HEAD_W] = oh[0:dec]


def _attn_sample(q, k_new, v_new, cache_k, cache_v, layer, page_table, rb_flat,
                 lq1, lk1, lq2, lk2, subln_g, dec, lam_init, pp):
    nseq, n_pages = page_table.shape
    depth, n_phys, page = cache_k.shape[:3]
    assert dec <= QROWS and page >= _FAR_DIST and n_pages % pp == 0
    n_steps = n_pages // pp
    q3 = q.reshape(nseq, dec, ATT_W)
    kn3 = k_new.reshape(nseq, dec * HEADS, HEAD_W)
    vn3 = v_new.reshape(nseq, dec * HEADS, HEAD_W)
    new_spec = pl.BlockSpec((1, dec * HEADS, HEAD_W), lambda b, p, pt: (b, 0, 0))
    prow = page * HEADS
    kc = cache_k.reshape(depth * n_phys * prow, HEAD_W)
    vc = cache_v.reshape(depth * n_phys * prow, HEAD_W)
    rows = 2 * HEADS * QROWS
    seq_spec = pl.BlockSpec((1, dec, ATT_W), lambda b, p, pt: (b, 0, 0))

    def page_spec(r):
        return pl.BlockSpec((prow, HEAD_W),
                            lambda b, p, pt: (layer * n_phys + pt[b, p * pp + r], 0))

    pages = [page_spec(r) for r in range(pp)]
    small = lambda n: pl.BlockSpec((1, n), lambda b, p, pt: (0, 0))
    grid_spec = pltpu.PrefetchScalarGridSpec(
        num_scalar_prefetch=1,
        grid=(nseq, n_steps),
        in_specs=[pl.BlockSpec(memory_space=pltpu.SMEM), seq_spec, new_spec, new_spec]
        + pages + pages
        + [small(QK_DIM), small(QK_DIM), small(QK_DIM), small(QK_DIM), small(HEAD_W)],
        out_specs=seq_spec,
        scratch_shapes=[pltpu.VMEM((QROWS, ATT_W), F32),
                        pltpu.VMEM((rows, HEAD_W), F32),
                        pltpu.VMEM((n_steps, rows, pp * prow), F32),
                        pltpu.VMEM((rows, 1), F32),
                        pltpu.VMEM((rows, 1), F32),
                        pltpu.VMEM((rows, HEAD_W), F32)])
    out = pl.pallas_call(
        functools.partial(_attn_sample_kernel, n_steps=n_steps, pp=pp, page=page, dec=dec,
                          lam_init=lam_init),
        grid_spec=grid_spec,
        out_shape=jax.ShapeDtypeStruct((nseq, dec, ATT_W), F32),
        compiler_params=_cparams(("arbitrary", "arbitrary")),
        name="attn_sample",
    )(page_table, rb_flat, q3, kn3, vn3, *([kc] * pp), *([vc] * pp), lq1, lk1, lq2, lk2, subln_g)
    return out.reshape(nseq * dec, ATT_W)


def _pool_groups(window_sum, u_of, cnt_of, pw_ref, ps_ref):
    outs = []
    for g, w in enumerate(POOL_WINDOWS):
        d = window_sum(g, w) / cnt_of(w) - u_of(g)
        outs.append(jnp.dot(d.astype(BF16), pw_ref[g], preferred_element_type=F32))
    return jnp.concatenate(outs, axis=-1) * ps_ref[...]


def _mix_prompt_kernel(x_ref, a_ref, u_ref, halo_ref, pw_ref, ps_ref, wo_ref, o_ref, z_ref,
                       *, tm, tiles_per_seq):
    i = pl.program_id(0)
    t_in_seq = i % tiles_per_seq
    z_ref[0:POOL_HALO, :] = jnp.where(t_in_seq == 0, 0.0, halo_ref[...])
    z_ref[POOL_HALO:POOL_HALO + tm, :] = u_ref[...]
    pos = t_in_seq * tm + lax.broadcasted_iota(jnp.int32, (tm, 1), 0)

    def window_sum(g, w):
        cs = slice(g * POOL_GW, (g + 1) * POOL_GW)
        acc = z_ref[POOL_HALO:POOL_HALO + tm, cs]
        for k in range(1, w):
            acc = acc + z_ref[POOL_HALO - k:POOL_HALO - k + tm, cs]
        return acc

    pool = _pool_groups(
        window_sum,
        lambda g: u_ref[:, g * POOL_GW:(g + 1) * POOL_GW],
        lambda w: jnp.minimum(pos + 1, w).astype(F32),
        pw_ref, ps_ref)
    mixed = jnp.concatenate([a_ref[...], pool], axis=-1).astype(BF16)
    o_ref[...] = x_ref[...] + jnp.dot(mixed, wo_ref[...], preferred_element_type=F32)


def _mix_prompt(x, attn, u, pool_w_bf, pool_scale, w_out_bf, seq, tm):
    t, d = x.shape
    pw_cols = u.shape[1]
    assert seq % tm == 0 and tm % POOL_HALO == 0
    halo_blocks = tm // POOL_HALO
    return pl.pallas_call(
        functools.partial(_mix_prompt_kernel, tm=tm, tiles_per_seq=seq // tm),
        grid=(t // tm,),
        in_specs=[pl.BlockSpec((tm, d), lambda i: (i, 0)),
                  pl.BlockSpec((tm, ATT_W), lambda i: (i, 0)),
                  pl.BlockSpec((tm, pw_cols), lambda i: (i, 0)),
                  pl.BlockSpec((POOL_HALO, pw_cols),
                               lambda i: (jnp.maximum(i * halo_blocks - 1, 0), 0)),
                  pl.BlockSpec(pool_w_bf.shape, lambda i: (0, 0, 0)),
                  pl.BlockSpec((1, pw_cols), lambda i: (0, 0)),
                  pl.BlockSpec(w_out_bf.shape, lambda i: (0, 0))],
        out_specs=pl.BlockSpec((tm, d), lambda i: (i, 0)),
        out_shape=jax.ShapeDtypeStruct((t, d), F32),
        scratch_shapes=[pltpu.VMEM((POOL_HALO + tm, pw_cols), F32)],
        compiler_params=_cparams(("arbitrary",)),
        name="mix_prompt",
    )(x, attn, u, u, pool_w_bf, pool_scale, w_out_bf)


def _mix_sample_kernel(x_ref, a_ref, u_ref, st_ref, pw_ref, ps_ref, wo_ref, o_ref,
                       *, dec, start_pos):
    def z_row(r):
        return st_ref[r] if r < POOL_PAD else u_ref[r - POOL_PAD]

    for i in range(dec):
        def window_sum(g, w, i=i):
            cs = slice(g * POOL_GW, (g + 1) * POOL_GW)
            acc = z_row(POOL_PAD + i)[:, cs]
            for k in range(1, w):
                acc = acc + z_row(POOL_PAD + i - k)[:, cs]
            return acc

        pool = _pool_groups(
            window_sum,
            lambda g, i=i: u_ref[i][:, g * POOL_GW:(g + 1) * POOL_GW],
            lambda w, i=i: float(min(start_pos + i + 1, w)),
            pw_ref, ps_ref)
        mixed = jnp.concatenate([a_ref[i], pool], axis=-1).astype(BF16)
        o_ref[i] = x_ref[i] + jnp.dot(mixed, wo_ref[...], preferred_element_type=F32)


def _mix_sample(x3, attn3, u3, state3, pool_w_bf, pool_scale, w_out_bf, start_pos):
    dec, nseq, d = x3.shape
    full = lambda a: pl.BlockSpec(a.shape, lambda i: (0,) * a.ndim)
    args = (x3, attn3, u3, state3, pool_w_bf, pool_scale, w_out_bf)
    return pl.pallas_call(
        functools.partial(_mix_sample_kernel, dec=dec, start_pos=start_pos),
        grid=(1,),
        in_specs=[full(a) for a in args],
        out_specs=full(x3),
        out_shape=jax.ShapeDtypeStruct((dec, nseq, d), F32),
        compiler_params=_cparams(("arbitrary",)),
        name="mix_sample",
    )(*args)


def _erf_gelu(x):
    return 0.5 * x * (1.0 + lax.erf(x * (2.0 ** -0.5)))


def _sort_network(n):
    pairs = []

    def merge(lo, cnt, r):
        step = 2 * r
        if step < cnt:
            merge(lo, cnt, step)
            merge(lo + r, cnt, step)
            pairs.extend((i, i + r) for i in range(lo + r, lo + cnt - r, step))
        else:
            pairs.append((lo, lo + r))

    def sort(lo, cnt):
        if cnt > 1:
            sort(lo, cnt // 2)
            sort(lo + cnt // 2, cnt // 2)
            merge(lo, cnt, 1)

    sort(0, n)
    return pairs


def _top_rows(x, n):
    sub = 8
    v = [x[i * sub:(i + 1) * sub] for i in range(x.shape[0] // sub)]
    size = 1 << (len(v) - 1).bit_length()
    v += [None] * (size - len(v))
    for i, j in _sort_network(size):
        if v[j] is None:
            continue
        if v[i] is None:
            v[i], v[j] = v[j], None
        else:
            v[i], v[j] = jnp.maximum(v[i], v[j]), jnp.minimum(v[i], v[j])
    v = [t for t in v if t is not None]
    rows = []
    for r in range(n):
        m = jnp.max(v[0], axis=0, keepdims=True)
        rows.append(m)
        left = n - r - 1
        if left:
            hit = v[0] == m
            for k in range(min(left, len(v))):
                nxt = v[k + 1] if k + 1 < len(v) else -jnp.inf
                v[k] = jnp.where(hit, nxt, v[k])
    return rows


def _rank_among(x, rows):
    rank = jnp.zeros(x.shape, F32)
    for q, row in enumerate(rows):
        rank = jnp.where(row > x, float(q + 1), rank)
    return rank


def _stack_rows(rows):
    n = len(rows)
    rid = lax.broadcasted_iota(jnp.int32, (n, LANES), 0)
    out = jnp.broadcast_to(rows[0], (n, LANES))
    for r in range(1, n):
        out = jnp.where(rid == r, rows[r], out)
    return out


def _peer_route_block(s1, s2):
    k = PEER_TOPK
    t1 = _top_rows(s1, k)
    t2 = _top_rows(s2, k)
    rank2 = _rank_among(s2, t2)
    t1s = _stack_rows(t1)
    t2s = _stack_rows(t2)
    half = k // 2
    rid = lax.broadcasted_iota(jnp.int32, (half, LANES), 0)
    cands = [t1s[0:half] + t2[0], t1s[half:k] + t2[0], t1s[0:half] + t2[1]]
    for q in range(2, half):
        cands.append(jnp.where(rid < k // (q + 1), t1s[0:half] + t2[q], -jnp.inf))
    cands.append(t2s[half:k] + t1[0])
    cand = jnp.concatenate(cands, axis=0)
    tau = _top_rows(cand, k)[k - 1]
    top = t1[0] + t2[0]
    z = jnp.sum(jnp.where(cand >= tau, jnp.exp(cand - top), 0.0), axis=0, keepdims=True)
    cnt = jnp.zeros(s1.shape, F32)
    for q in range(k):
        cnt = jnp.where(s1 + t2[q] >= tau, float(q + 1), cnt)
    a = jnp.exp(s1 - t1[0])
    b = jnp.exp(s2 - t2[0]) / z
    return cnt, rank2, a, b


def _peer_kernel(x_ref, g_ref, wq_ref, keys_ref, u_ref, vt_ref, o_ref,
                 xn_ref, s1_ref, s2_ref, ca_ref, rb_ref, w_ref, acc_ref,
                 *, tt, ec):
    j = pl.program_id(1)
    n_chunks = pl.num_programs(1)
    nsb = tt // LANES
    c_per = ec // PEER_KEYS
    pk = BF16_SUBLANES

    n_pieces = nsb * PEER_KEYS // GATE_ROWS

    def gates(chunk, w_ref, pieces=range(n_pieces)):
        zero = jnp.zeros((pk, LANES), BF16)
        c0 = pl.multiple_of(chunk * c_per, c_per)
        gpp = GATE_ROWS // pk
        for piece in pieces:
            sb, r0 = piece // (PEER_KEYS // GATE_ROWS), (piece % (PEER_KEYS // GATE_ROWS)) * GATE_ROWS
            w = [[zero] * gpp for _ in range(c_per)]
            for h in range(PEER_HEADS):
                cnt = [jnp.broadcast_to(ca_ref[h, sb, pl.ds(c0 + cl, 1), :], (pk, LANES)).astype(BF16)
                       for cl in range(c_per)]
                a = [jnp.broadcast_to(ca_ref[h, sb, pl.ds(A_OFF + c0 + cl, 1), :],
                                      (pk, LANES)).astype(BF16) for cl in range(c_per)]
                for g in range(gpp):
                    lo = r0 + g * pk
                    rank = rb_ref[h, sb, lo:lo + pk, :]
                    b = rb_ref[h, sb, B_OFF + lo:B_OFF + lo + pk, :]
                    for cl in range(c_per):
                        w[cl][g] = w[cl][g] + jnp.where(rank < cnt[cl], b * a[cl], zero)
            for cl in range(c_per):
                for g in range(gpp):
                    e0 = cl * PEER_KEYS + r0 + g * pk
                    w_ref[e0:e0 + pk, sb * LANES:(sb + 1) * LANES] = w[cl][g]

    @pl.when(j == 0)
    def _route():
        xn = _rmsnorm_rows(x_ref[...], g_ref[...]).astype(BF16)
        xn_ref[...] = xn

        def head_scores(h):
            qh = jnp.dot(xn_ref[...], wq_ref[h], preferred_element_type=F32)
            half = qh.shape[1] // 2
            for part, dst in ((0, s1_ref), (1, s2_ref)):
                qp = qh[:, part * half:(part + 1) * half].astype(BF16)
                st = lax.dot_general(keys_ref[h, part], qp, (((1,), (1,)), ((), ())),
                                     preferred_element_type=F32)
                for sb in range(nsb):
                    dst[h, sb] = st[:, sb * LANES:(sb + 1) * LANES]

        def route(idx):
            h, sb = idx // nsb, idx % nsb
            cnt, rank2, a, b = _peer_route_block(s1_ref[h, sb], s2_ref[h, sb])
            ca_ref[h, sb, 0:PEER_KEYS, :] = cnt
            ca_ref[h, sb, A_OFF:A_OFF + PEER_KEYS, :] = a
            rb_ref[h, sb, 0:PEER_KEYS, :] = rank2.astype(BF16)
            rb_ref[h, sb, B_OFF:B_OFF + PEER_KEYS, :] = b.astype(BF16)

        def per_trip(fn, n):
            def body(i, carry):
                for s in range(n):
                    fn(n * i + s)
                return carry
            return body

        lax.fori_loop(0, PEER_HEADS, per_trip(head_scores, 1), 0)
        lax.fori_loop(0, PEER_HEADS * nsb // 2, per_trip(route, 2), 0)
        acc_ref[...] = jnp.zeros(acc_ref.shape, F32)

    gates(j, w_ref)
    at = lax.dot_general(u_ref[...], xn_ref[...], (((1,), (1,)), ((), ())),
                         preferred_element_type=F32)
    p = w_ref[...] * _erf_gelu(at).astype(BF16)
    acc_ref[...] += jnp.dot(vt_ref[...], p, preferred_element_type=F32)

    @pl.when(j == n_chunks - 1)
    def _finish():
        o_ref[...] = x_ref[...] + acc_ref[...].T


def _peer(x, g, wq_heads_bf, keys_bf, u_bf, vt_bf, tt, ec):
    t, d = x.shape
    n_exp = u_bf.shape[0]
    assert t % tt == 0 and n_exp % ec == 0 and tt % LANES == 0 and ec % PEER_KEYS == 0
    assert n_exp == PEER_KEYS * PEER_KEYS
    nsb = tt // LANES
    blk = lambda dt: pltpu.VMEM((PEER_HEADS, nsb, PEER_KEYS, LANES), dt)
    return pl.pallas_call(
        functools.partial(_peer_kernel, tt=tt, ec=ec),
        grid=(t // tt, n_exp // ec),
        in_specs=[pl.BlockSpec((tt, d), lambda i, j: (i, 0)),
                  pl.BlockSpec((1, d), lambda i, j: (0, 0)),
                  pl.BlockSpec(wq_heads_bf.shape, lambda i, j: (0, 0, 0)),
                  pl.BlockSpec(keys_bf.shape, lambda i, j: (0, 0, 0, 0)),
                  pl.BlockSpec((ec, d), lambda i, j: (j, 0)),
                  pl.BlockSpec((d, ec), lambda i, j: (0, j))],
        out_specs=pl.BlockSpec((tt, d), lambda i, j: (i, 0)),
        out_shape=jax.ShapeDtypeStruct((t, d), F32),
        scratch_shapes=[pltpu.VMEM((tt, d), BF16),
                        blk(F32), blk(F32),
                        pltpu.VMEM((PEER_HEADS, nsb, A_OFF + PEER_KEYS, LANES), F32),
                        pltpu.VMEM((PEER_HEADS, nsb, B_OFF + PEER_KEYS, LANES), BF16),
                        pltpu.VMEM((ec, tt), BF16),
                        pltpu.VMEM((d, tt), F32)],
        compiler_params=_cparams(("arbitrary", "arbitrary")),
        name="peer",
    )(x, g, wq_heads_bf, keys_bf, u_bf, vt_bf)


def _tail_kernel(x_ref, p_ref, gpl_ref, wg_ref, wpl_ref, gf_ref, o_ref):
    x = x_ref[...]
    xn = _rmsnorm_rows(x, gpl_ref[...]).astype(BF16)
    gate = jax.nn.sigmoid(jnp.dot(xn, wg_ref[...], preferred_element_type=F32))
    emb = jnp.dot(p_ref[...].astype(BF16), wpl_ref[...], preferred_element_type=F32)
    o_ref[...] = _rmsnorm_rows(x + gate * emb, gf_ref[...])


def _tail(x, p, g_pl, w_gate_bf, w_pl_bf, g_final, tm):
    t, d = x.shape
    pd = p.shape[1]
    return pl.pallas_call(
        _tail_kernel,
        grid=(t // tm,),
        in_specs=[pl.BlockSpec((tm, d), lambda i: (i, 0)),
                  pl.BlockSpec((tm, pd), lambda i: (i, 0)),
                  pl.BlockSpec((1, d), lambda i: (0, 0)),
                  pl.BlockSpec((d, d), lambda i: (0, 0)),
                  pl.BlockSpec((pd, d), lambda i: (0, 0)),
                  pl.BlockSpec((1, d), lambda i: (0, 0))],
        out_specs=pl.BlockSpec((tm, d), lambda i: (i, 0)),
        out_shape=jax.ShapeDtypeStruct((t, d), F32),
        compiler_params=_cparams(("arbitrary",)),
        name="tail",
    )(x, p, g_pl, w_gate_bf, w_pl_bf, g_final)


def _row_tile(t, want):
    tm = min(t, want)
    assert t % tm == 0
    return tm


def kernel(x_prompt, x_sample, cache_k, cache_v, page_table, state_pool, p_prompt, p_sample, norm_attn_g, w_in, lambda_q1, lambda_k1, lambda_q2, lambda_k2, subln_g, rel_bias, pool_w, pool_scale, w_out, norm_ffn_g, peer_wq, peer_keys, peer_u, peer_v, norm_pl_g, w_pl, w_pl_gate, final_norm_g):
    depth = w_in.shape[0]
    assert depth == 1
    batch, seq, d = x_prompt.shape
    nseq, dec, _ = x_sample.shape
    n_pages = page_table.shape[1]
    page = cache_k.shape[2]
    past_len = n_pages * page
    lam_init = 0.8 - 0.6 * math.exp(-0.3 * 0)
    i = 0

    row = lambda a: a.reshape(1, -1)
    w_in_bf = w_in[i].astype(BF16)
    pool_w_bf = pool_w[i].astype(BF16)
    w_out_bf = w_out[i].astype(BF16)
    qd = peer_wq.shape[2] // PEER_HEADS
    wq_heads_bf = peer_wq[i].reshape(d, PEER_HEADS, qd).transpose(1, 0, 2).astype(BF16)
    keys_bf = peer_keys[i].astype(BF16)
    u_bf = peer_u[i].astype(BF16)
    vt_bf = peer_v[i].astype(BF16).T
    w_gate_bf = w_pl_gate[i].astype(BF16)
    w_pl_bf = w_pl[i].astype(BF16)
    rb_flat = rel_bias.reshape(-1)
    lam_rows = (row(lambda_q1[i]), row(lambda_k1[i]), row(lambda_q2[i]), row(lambda_k2[i]))
    sg = row(subln_g[i])

    tp = batch * seq
    xp = x_prompt.reshape(tp, d)
    _, k, v, u, qb, kb, vtb = _inproj(xp, row(norm_attn_g[i]), w_in_bf, _row_tile(tp, 512))
    attn = _attn_prompt(qb, kb, vtb, rb_flat, *lam_rows, subln_g[i].reshape(-1, 1),
                        batch, seq, min(seq, 512), lam_init)
    x1 = _mix_prompt(xp, attn, u, pool_w_bf, row(pool_scale[i]), w_out_bf, seq, min(seq, 512))
    x2 = _peer(x1, row(norm_ffn_g[i]), wq_heads_bf, keys_bf, u_bf, vt_bf,
               _row_tile(tp, 512), min(u_bf.shape[0], 1024))
    y_prompt = _tail(x2, p_prompt[i].reshape(tp, -1), row(norm_pl_g[i]), w_gate_bf, w_pl_bf,
                     row(final_norm_g), _row_tile(tp, 512)).reshape(batch, seq, d)
    new_k_prompt = k.reshape(1, batch, seq, HEADS, HEAD_W)
    new_v_prompt = v.reshape(1, batch, seq, HEADS, HEAD_W)
    new_pool_prompt = u.reshape(batch, seq, -1)[:, seq - POOL_PAD:, :][None]

    ts = nseq * dec
    xs = x_sample.reshape(ts, d)
    qs, ks, vs, us, _, _, _ = _inproj(xs, row(norm_attn_g[i]), w_in_bf, _row_tile(ts, 512))
    attn_s = _attn_sample(qs, ks, vs, cache_k, cache_v, i, page_table, rb_flat,
                          *lam_rows, sg, dec, lam_init, math.gcd(n_pages, 16))
    tok_major = lambda a: a.reshape(nseq, dec, -1).transpose(1, 0, 2)
    x1s = _mix_sample(tok_major(xs), tok_major(attn_s), tok_major(us),
                      state_pool[i].transpose(1, 0, 2), pool_w_bf, row(pool_scale[i]),
                      w_out_bf, past_len)
    x1s = x1s.transpose(1, 0, 2).reshape(ts, d)
    x2s = _peer(x1s, row(norm_ffn_g[i]), wq_heads_bf, keys_bf, u_bf, vt_bf,
                _row_tile(ts, 512), min(u_bf.shape[0], 1024))
    y_sample = _tail(x2s, p_sample[i].reshape(ts, -1), row(norm_pl_g[i]), w_gate_bf, w_pl_bf,
                     row(final_norm_g), _row_tile(ts, 512)).reshape(nseq, dec, d)
    new_k_sample = ks.reshape(1, nseq, dec, HEADS, HEAD_W)
    new_v_sample = vs.reshape(1, nseq, dec, HEADS, HEAD_W)
    new_pool_sample = jnp.concatenate(
        [state_pool[i][:, dec:, :], us.reshape(nseq, dec, -1)], axis=1)[None]

    return (y_prompt, y_sample, new_k_prompt, new_v_prompt, new_pool_prompt,
            new_k_sample, new_v_sample, new_pool_sample)
```

```python
import functools
import math

import jax
import jax.numpy as jnp
import numpy as np
from jax import lax
from jax.experimental import pallas as pl
from jax.experimental.pallas import tpu as pltpu

F32 = jnp.float32
BF16 = jnp.bfloat16

EPS = 1e-6
NEG_INF = -1e30
LOG2E = 1.4426950408889634
HEADS = 4
QK_DIM = 64
HEAD_W = 2 * QK_DIM
ATT_W = HEADS * HEAD_W
POOL_WINDOWS = (2, 4, 8, 16)
POOL_GW = 128
POOL_PAD = 15
POOL_HALO = 16
REL_BUCKETS = 32
REL_MAX_DIST = 128
PEER_HEADS = 8
PEER_KEYS = 128
PEER_TOPK = 16
LANES = 128
BF16_SUBLANES = 16
GATE_ROWS = 32
GATE_EXPERT_ROWS = 8
PEER_EC = 2048
A_OFF = PEER_KEYS + 8
B_OFF = PEER_KEYS + BF16_SUBLANES
ONES_ROWS = 16
VMEM_LIMIT = 56 * 1024 * 1024


def _bucket_thresholds():
    exact = REL_BUCKETS // 2
    n = np.arange(0, 4 * REL_MAX_DIST)
    large = exact + (np.log(np.maximum(n, 1).astype(np.float32) / np.float32(exact))
                     / np.float32(math.log(REL_MAX_DIST / exact))
                     * np.float32(REL_BUCKETS - exact)).astype(np.int32)
    large = np.minimum(large, REL_BUCKETS - 1)
    bucket = np.where(n < exact, n, large)
    return [int(n[np.argmax(bucket >= b)]) for b in range(REL_BUCKETS)]


_BUCKET_START = _bucket_thresholds()
_FAR_DIST = _BUCKET_START[REL_BUCKETS - 1]


def _rel_bias_tile(rel, rb_ref, h):
    out = jnp.full(rel.shape, rb_ref[h], F32)
    for b in range(1, REL_BUCKETS):
        out = jnp.where(rel >= _BUCKET_START[b], rb_ref[b * HEADS + h], out)
    return out


def _rmsnorm_rows(x, g):
    return x * lax.rsqrt(jnp.mean(x * x, axis=-1, keepdims=True) + EPS) * g


def _cparams(sem):
    return pltpu.CompilerParams(dimension_semantics=sem, vmem_limit_bytes=VMEM_LIMIT)


def _inproj_kernel(x_ref, g_ref, w_ref, q_ref, k_ref, v_ref, u_ref, qb_ref, kb_ref, vtb_ref):
    xn = _rmsnorm_rows(x_ref[...], g_ref[...])
    proj = jnp.dot(xn.astype(BF16), w_ref[...], preferred_element_type=F32)
    q = proj[:, 0 * ATT_W:1 * ATT_W]
    k = proj[:, 1 * ATT_W:2 * ATT_W]
    v = proj[:, 2 * ATT_W:3 * ATT_W]
    q_ref[...] = q
    tm = q.shape[0]
    for h in range(HEADS):
        k_ref[pl.ds(h, tm, stride=HEADS), :] = k[:, h * HEAD_W:(h + 1) * HEAD_W]
        v_ref[pl.ds(h, tm, stride=HEADS), :] = v[:, h * HEAD_W:(h + 1) * HEAD_W]
    u_ref[...] = proj[:, 3 * ATT_W:4 * ATT_W]
    qb_ref[...] = (q * (QK_DIM ** -0.5 * LOG2E)).astype(BF16)
    kb_ref[...] = k.astype(BF16)
    vtb_ref[...] = v.T.astype(BF16)


def _inproj(x, g, w_bf, tm):
    t, d = x.shape
    n_out = w_bf.shape[1]
    assert n_out == 4 * ATT_W and t % tm == 0
    out = jax.ShapeDtypeStruct((t, ATT_W), F32)
    out_kv = jax.ShapeDtypeStruct((t * HEADS, HEAD_W), F32)
    out_bf = jax.ShapeDtypeStruct((t, ATT_W), BF16)
    row = pl.BlockSpec((tm, ATT_W), lambda i: (i, 0))
    row_kv = pl.BlockSpec((tm * HEADS, HEAD_W), lambda i: (i, 0))
    return pl.pallas_call(
        _inproj_kernel,
        grid=(t // tm,),
        in_specs=[pl.BlockSpec((tm, d), lambda i: (i, 0)),
                  pl.BlockSpec((1, d), lambda i: (0, 0)),
                  pl.BlockSpec((d, n_out), lambda i: (0, 0))],
        out_specs=[row, row_kv, row_kv, row, row, row, pl.BlockSpec((ATT_W, tm), lambda i: (0, i))],
        out_shape=[out, out_kv, out_kv, out, out_bf, out_bf,
                   jax.ShapeDtypeStruct((ATT_W, t), BF16)],
        compiler_params=_cparams(("arbitrary",)),
        name="inproj",
    )(x, g, w_bf)


def _lambda_full(lq1, lk1, lq2, lk2, lam_init):
    return (jnp.exp(jnp.sum(lq1 * lk1, axis=-1, keepdims=True))
            - jnp.exp(jnp.sum(lq2 * lk2, axis=-1, keepdims=True)) + lam_init)


def _head_out(acc1, l1, acc2, l2, lam, subln_g, lam_init):
    o = acc1 / l1 - lam * (acc2 / l2)
    return _rmsnorm_rows(o, subln_g) * (1.0 - lam_init)


def _attn_prompt_kernel(rb_ref, q_ref, k_ref, vt_ref, lq1_ref, lk1_ref, lq2_ref, lk2_ref, sg_ref,
                        o_ref, bias_ref, ve_ref, m_ref, acc_ref, *, tb, lam_init):
    b, qi, kj = pl.program_id(0), pl.program_id(1), pl.program_id(2)

    @pl.when((b == 0) & (qi == 0) & (kj == 0))
    def _build_constants():
        key = lax.broadcasted_iota(jnp.int32, (tb, tb), 0)
        qry = lax.broadcasted_iota(jnp.int32, (tb, tb), 1)
        rel = qry - key
        for h in range(HEADS):
            diag = _rel_bias_tile(rel, rb_ref, h) * LOG2E
            bias_ref[0, h] = jnp.where(rel >= 0, diag, NEG_INF)
            bias_ref[1, h] = _rel_bias_tile(rel + tb, rb_ref, h) * LOG2E
            ve_ref[h, HEAD_W:HEAD_W + ONES_ROWS, :] = jnp.ones((ONES_ROWS, tb), ve_ref.dtype)

    @pl.when(kj == 0)
    def _init():
        m_ref[...] = jnp.full(m_ref.shape, -jnp.inf, F32)
        acc_ref[...] = jnp.zeros(acc_ref.shape, F32)

    def step(near):
        for h in range(HEADS):
            ve_ref[h, 0:HEAD_W, :] = vt_ref[h * HEAD_W:(h + 1) * HEAD_W, :]
        for h in range(HEADS):
            for mp in range(2):
                c0 = h * HEAD_W + mp * QK_DIM
                idx = h * 2 + mp
                st = lax.dot_general(k_ref[:, c0:c0 + QK_DIM], q_ref[:, c0:c0 + QK_DIM],
                                     (((1,), (1,)), ((), ())), preferred_element_type=F32)
                m_old = m_ref[idx]
                if near:
                    st = st + bias_ref[jnp.where(kj == qi, 0, 1), h]
                    m_new = jnp.maximum(m_old, jnp.max(st, axis=0, keepdims=True))
                    shift = m_new
                else:
                    far = rb_ref[(REL_BUCKETS - 1) * HEADS + h] * LOG2E
                    m_new = jnp.maximum(m_old, jnp.max(st, axis=0, keepdims=True) + far)
                    shift = m_new - far
                alpha = jnp.exp2(m_old - m_new)
                p = jnp.exp2(st - shift).astype(BF16)
                acc_ref[idx] = alpha * acc_ref[idx] + jnp.dot(
                    ve_ref[h], p, preferred_element_type=F32)
                m_ref[idx] = m_new

    @pl.when((kj <= qi) & (kj >= qi - 1))
    def _near_step():
        step(True)

    @pl.when(kj < qi - 1)
    def _far_step():
        step(False)

    @pl.when(kj == qi)
    def _finish():
        lam = _lambda_full(lq1_ref[...], lk1_ref[...], lq2_ref[...], lk2_ref[...], lam_init)
        for h in range(HEADS):
            a1, a2 = acc_ref[2 * h], acc_ref[2 * h + 1]
            o = (a1[0:HEAD_W] / a1[HEAD_W:HEAD_W + 1]
                 - lam * (a2[0:HEAD_W] / a2[HEAD_W:HEAD_W + 1]))
            ms = jnp.mean(o * o, axis=0, keepdims=True)
            on = o * lax.rsqrt(ms + EPS) * sg_ref[...] * (1.0 - lam_init)
            o_ref[:, h * HEAD_W:(h + 1) * HEAD_W] = on.T


def _attn_prompt(q, k, vt, rb_flat, lq1, lk1, lq2, lk2, subln_col, batch, seq, tb, lam_init):
    t = q.shape[0]
    assert t == batch * seq and seq % tb == 0 and tb >= _FAR_DIST and q.dtype == BF16
    nq = seq // tb
    small = lambda n: pl.BlockSpec((1, n), lambda b, i, j: (0, 0))
    k_spec = pl.BlockSpec((tb, ATT_W), lambda b, i, j: (b * nq + jnp.minimum(j, i), 0))
    vt_spec = pl.BlockSpec((ATT_W, tb), lambda b, i, j: (0, b * nq + jnp.minimum(j, i)))
    q_spec = pl.BlockSpec((tb, ATT_W), lambda b, i, j: (b * nq + i, 0))
    return pl.pallas_call(
        functools.partial(_attn_prompt_kernel, tb=tb, lam_init=lam_init),
        grid=(batch, nq, nq),
        in_specs=[pl.BlockSpec(memory_space=pltpu.SMEM),
                  q_spec, k_spec, vt_spec,
                  small(QK_DIM), small(QK_DIM), small(QK_DIM), small(QK_DIM),
                  pl.BlockSpec((HEAD_W, 1), lambda b, i, j: (0, 0))],
        out_specs=q_spec,
        out_shape=jax.ShapeDtypeStruct((t, ATT_W), F32),
        scratch_shapes=[pltpu.VMEM((2, HEADS, tb, tb), F32),
                        pltpu.VMEM((HEADS, HEAD_W + ONES_ROWS, tb), BF16),
                        pltpu.VMEM((2 * HEADS, 1, tb), F32),
                        pltpu.VMEM((2 * HEADS, HEAD_W + ONES_ROWS, tb), F32)],
        compiler_params=_cparams(("arbitrary", "arbitrary", "arbitrary")),
        name="attn_prompt",
    )(rb_flat, q, k, vt, lq1, lk1, lq2, lk2, subln_col)


QROWS = 8


def _attn_sample_kernel(pt_ref, rb_ref, q_ref, kn_ref, vn_ref, *rest,
                        n_steps, pp, page, dec, lam_init):
    del pt_ref
    k_refs, v_refs = rest[:pp], rest[pp:2 * pp]
    (lq1_ref, lk1_ref, lq2_ref, lk2_ref, sg_ref, o_ref,
     q8_ref, qcat_ref, bias_ref, m_ref, l_ref, acc_ref) = rest[2 * pp:]
    b, p = pl.program_id(0), pl.program_id(1)
    hrows = 2 * QROWS
    rows = HEADS * hrows
    prow = page * HEADS
    width = pp * prow
    past_len = n_steps * pp * page
    row_i = lax.broadcasted_iota(jnp.int32, (rows, 1), 0) % QROWS

    def per_head(fn):
        return jnp.concatenate([fn(h) for h in range(HEADS)], axis=0)

    @pl.when(b == 0)
    def _build_bias():
        col = lax.broadcasted_iota(jnp.int32, (hrows, width), 1)
        k_pos = p * (pp * page) + col // HEADS
        rel = past_len + lax.broadcasted_iota(jnp.int32, (hrows, width), 0) % QROWS - k_pos
        bias_ref[p] = per_head(lambda h: jnp.where(col % HEADS == h,
                                                   _rel_bias_tile(rel, rb_ref, h), NEG_INF))

    @pl.when(p == 0)
    def _init():
        q8_ref[...] = jnp.zeros(q8_ref.shape, F32)
        q8_ref[0:dec, :] = q_ref[0] * (QK_DIM ** -0.5)
        col = lax.broadcasted_iota(jnp.int32, (QROWS, HEAD_W), 1)
        for h in range(HEADS):
            qh = q8_ref[:, h * HEAD_W:(h + 1) * HEAD_W]
            for mp in range(2):
                r0 = h * hrows + mp * QROWS
                qcat_ref[r0:r0 + QROWS, :] = jnp.where(col // QK_DIM == mp, qh, 0.0)
        m_ref[...] = jnp.full(m_ref.shape, -jnp.inf, F32)
        l_ref[...] = jnp.zeros(l_ref.shape, F32)
        acc_ref[...] = jnp.zeros(acc_ref.shape, F32)

    qcat = qcat_ref[...]
    qcat_bf = qcat.astype(BF16)
    s = jnp.concatenate(
        [lax.dot_general(qcat_bf, k_refs[r][...].astype(BF16), (((1,), (1,)), ((), ())),
                         preferred_element_type=F32) for r in range(pp)], axis=1)
    s = s + bias_ref[p]
    m_old = m_ref[...]
    m_new = jnp.maximum(m_old, jnp.max(s, axis=-1, keepdims=True))
    alpha = jnp.exp(m_old - m_new)
    pr = jnp.exp(s - m_new)
    l_ref[...] = alpha * l_ref[...] + jnp.sum(pr, axis=-1, keepdims=True)
    pv = jnp.dot(pr[:, 0:prow].astype(BF16), v_refs[0][...].astype(BF16),
                 preferred_element_type=F32)
    for r in range(1, pp):
        pv = pv + jnp.dot(pr[:, r * prow:(r + 1) * prow].astype(BF16),
                          v_refs[r][...].astype(BF16), preferred_element_type=F32)
    acc_ref[...] = alpha * acc_ref[...] + pv
    m_ref[...] = m_new

    @pl.when(p == n_steps - 1)
    def _finish():
        def head_row(ref, j, h):
            return ref[0, j * HEADS + h:j * HEADS + h + 1, :]

        s_new = []
        for j in range(dec):
            sj = per_head(lambda h, j=j: jnp.sum(
                qcat[h * hrows:(h + 1) * hrows] * head_row(kn_ref, j, h), axis=-1, keepdims=True)
                + _rel_bias_tile(row_i[0:hrows] - j, rb_ref, h))
            s_new.append(jnp.where(row_i >= j, sj, NEG_INF))
        m_old2 = m_ref[...]
        m_fin = m_old2
        for sj in s_new:
            m_fin = jnp.maximum(m_fin, sj)
        alpha2 = jnp.exp(m_old2 - m_fin)
        l_fin = alpha2 * l_ref[...]
        acc_fin = alpha2 * acc_ref[...]
        for j, sj in enumerate(s_new):
            pj = jnp.exp(sj - m_fin)
            l_fin = l_fin + pj
            acc_fin = acc_fin + pj * per_head(
                lambda h, j=j: jnp.broadcast_to(head_row(vn_ref, j, h), (hrows, HEAD_W)))
        lam = _lambda_full(lq1_ref[...], lk1_ref[...], lq2_ref[...], lk2_ref[...], lam_init)
        for h in range(HEADS):
            r1, r2 = h * hrows, h * hrows + QROWS
            oh = _head_out(acc_fin[r1:r1 + QROWS], l_fin[r1:r1 + QROWS],
                           acc_fin[r2:r2 + QROWS], l_fin[r2:r2 + QROWS],
                           lam, sg_ref[...], lam_init)
            o_ref[0, :, h * HEAD_W:(h + 1) * HEAD_W] = oh[0:dec]


def _attn_sample(q, k_new, v_new, cache_k, cache_v, layer, page_table, rb_flat,
                 lq1, lk1, lq2, lk2, subln_g, dec, lam_init, pp):
    nseq, n_pages = page_table.shape
    depth, n_phys, page = cache_k.shape[:3]
    assert dec <= QROWS and page >= _FAR_DIST and n_pages % pp == 0
    n_steps = n_pages // pp
    q3 = q.reshape(nseq, dec, ATT_W)
    kn3 = k_new.reshape(nseq, dec * HEADS, HEAD_W)
    vn3 = v_new.reshape(nseq, dec * HEADS, HEAD_W)
    new_spec = pl.BlockSpec((1, dec * HEADS, HEAD_W), lambda b, p, pt: (b, 0, 0))
    prow = page * HEADS
    kc = cache_k.reshape(depth * n_phys * prow, HEAD_W)
    vc = cache_v.reshape(depth * n_phys * prow, HEAD_W)
    rows = 2 * HEADS * QROWS
    seq_spec = pl.BlockSpec((1, dec, ATT_W), lambda b, p, pt: (b, 0, 0))

    def page_spec(r):
        return pl.BlockSpec((prow, HEAD_W),
                            lambda b, p, pt: (layer * n_phys + pt[b, p * pp + r], 0))

    pages = [page_spec(r) for r in range(pp)]
    small = lambda n: pl.BlockSpec((1, n), lambda b, p, pt: (0, 0))
    grid_spec = pltpu.PrefetchScalarGridSpec(
        num_scalar_prefetch=1,
        grid=(nseq, n_steps),
        in_specs=[pl.BlockSpec(memory_space=pltpu.SMEM), seq_spec, new_spec, new_spec]
        + pages + pages
        + [small(QK_DIM), small(QK_DIM), small(QK_DIM), small(QK_DIM), small(HEAD_W)],
        out_specs=seq_spec,
        scratch_shapes=[pltpu.VMEM((QROWS, ATT_W), F32),
                        pltpu.VMEM((rows, HEAD_W), F32),
                        pltpu.VMEM((n_steps, rows, pp * prow), F32),
                        pltpu.VMEM((rows, 1), F32),
                        pltpu.VMEM((rows, 1), F32),
                        pltpu.VMEM((rows, HEAD_W), F32)])
    out = pl.pallas_call(
        functools.partial(_attn_sample_kernel, n_steps=n_steps, pp=pp, page=page, dec=dec,
                          lam_init=lam_init),
        grid_spec=grid_spec,
        out_shape=jax.ShapeDtypeStruct((nseq, dec, ATT_W), F32),
        compiler_params=_cparams(("arbitrary", "arbitrary")),
        name="attn_sample",
    )(page_table, rb_flat, q3, kn3, vn3, *([kc] * pp), *([vc] * pp), lq1, lk1, lq2, lk2, subln_g)
    return out.reshape(nseq * dec, ATT_W)


def _pool_groups(window_sum, u_of, cnt_of, pw_ref, ps_ref):
    outs = []
    for g, w in enumerate(POOL_WINDOWS):
        d = window_sum(g, w) / cnt_of(w) - u_of(g)
        outs.append(jnp.dot(d.astype(BF16), pw_ref[g], preferred_element_type=F32))
    return jnp.concatenate(outs, axis=-1) * ps_ref[...]


def _mix_prompt_kernel(x_ref, a_ref, u_ref, halo_ref, pw_ref, ps_ref, wo_ref, o_ref, z_ref,
                       *, tm, tiles_per_seq):
    i = pl.program_id(0)
    t_in_seq = i % tiles_per_seq
    z_ref[0:POOL_HALO, :] = jnp.where(t_in_seq == 0, 0.0, halo_ref[...])
    z_ref[POOL_HALO:POOL_HALO + tm, :] = u_ref[...]
    pos = t_in_seq * tm + lax.broadcasted_iota(jnp.int32, (tm, 1), 0)

    def window_sum(g, w):
        cs = slice(g * POOL_GW, (g + 1) * POOL_GW)
        acc = z_ref[POOL_HALO:POOL_HALO + tm, cs]
        for k in range(1, w):
            acc = acc + z_ref[POOL_HALO - k:POOL_HALO - k + tm, cs]
        return acc

    pool = _pool_groups(
        window_sum,
        lambda g: u_ref[:, g * POOL_GW:(g + 1) * POOL_GW],
        lambda w: jnp.minimum(pos + 1, w).astype(F32),
        pw_ref, ps_ref)
    mixed = jnp.concatenate([a_ref[...], pool], axis=-1).astype(BF16)
    o_ref[...] = x_ref[...] + jnp.dot(mixed, wo_ref[...], preferred_element_type=F32)


def _mix_prompt(x, attn, u, pool_w_bf, pool_scale, w_out_bf, seq, tm):
    t, d = x.shape
    pw_cols = u.shape[1]
    assert seq % tm == 0 and tm % POOL_HALO == 0
    halo_blocks = tm // POOL_HALO
    return pl.pallas_call(
        functools.partial(_mix_prompt_kernel, tm=tm, tiles_per_seq=seq // tm),
        grid=(t // tm,),
        in_specs=[pl.BlockSpec((tm, d), lambda i: (i, 0)),
                  pl.BlockSpec((tm, ATT_W), lambda i: (i, 0)),
                  pl.BlockSpec((tm, pw_cols), lambda i: (i, 0)),
                  pl.BlockSpec((POOL_HALO, pw_cols),
                               lambda i: (jnp.maximum(i * halo_blocks - 1, 0), 0)),
                  pl.BlockSpec(pool_w_bf.shape, lambda i: (0, 0, 0)),
                  pl.BlockSpec((1, pw_cols), lambda i: (0, 0)),
                  pl.BlockSpec(w_out_bf.shape, lambda i: (0, 0))],
        out_specs=pl.BlockSpec((tm, d), lambda i: (i, 0)),
        out_shape=jax.ShapeDtypeStruct((t, d), F32),
        scratch_shapes=[pltpu.VMEM((POOL_HALO + tm, pw_cols), F32)],
        compiler_params=_cparams(("arbitrary",)),
        name="mix_prompt",
    )(x, attn, u, u, pool_w_bf, pool_scale, w_out_bf)


def _mix_sample_kernel(x_ref, a_ref, u_ref, st_ref, pw_ref, ps_ref, wo_ref, o_ref,
                       *, dec, start_pos):
    def z_row(r):
        return st_ref[r] if r < POOL_PAD else u_ref[r - POOL_PAD]

    for i in range(dec):
        def window_sum(g, w, i=i):
            cs = slice(g * POOL_GW, (g + 1) * POOL_GW)
            acc = z_row(POOL_PAD + i)[:, cs]
            for k in range(1, w):
                acc = acc + z_row(POOL_PAD + i - k)[:, cs]
            return acc

        pool = _pool_groups(
            window_sum,
            lambda g, i=i: u_ref[i][:, g * POOL_GW:(g + 1) * POOL_GW],
            lambda w, i=i: float(min(start_pos + i + 1, w)),
            pw_ref, ps_ref)
        mixed = jnp.concatenate([a_ref[i], pool], axis=-1).astype(BF16)
        o_ref[i] = x_ref[i] + jnp.dot(mixed, wo_ref[...], preferred_element_type=F32)


def _mix_sample(x3, attn3, u3, state3, pool_w_bf, pool_scale, w_out_bf, start_pos):
    dec, nseq, d = x3.shape
    full = lambda a: pl.BlockSpec(a.shape, lambda i: (0,) * a.ndim)
    args = (x3, attn3, u3, state3, pool_w_bf, pool_scale, w_out_bf)
    return pl.pallas_call(
        functools.partial(_mix_sample_kernel, dec=dec, start_pos=start_pos),
        grid=(1,),
        in_specs=[full(a) for a in args],
        out_specs=full(x3),
        out_shape=jax.ShapeDtypeStruct((dec, nseq, d), F32),
        compiler_params=_cparams(("arbitrary",)),
        name="mix_sample",
    )(*args)


def _erf_gelu_x2(x):
    return x + x * lax.erf(x * (2.0 ** -0.5))


def _sort_network(n):
    pairs = []

    def merge(lo, cnt, r):
        step = 2 * r
        if step < cnt:
            merge(lo, cnt, step)
            merge(lo + r, cnt, step)
            pairs.extend((i, i + r) for i in range(lo + r, lo + cnt - r, step))
        else:
            pairs.append((lo, lo + r))

    def sort(lo, cnt):
        if cnt > 1:
            sort(lo, cnt // 2)
            sort(lo + cnt // 2, cnt // 2)
            merge(lo, cnt, 1)

    sort(0, n)
    return pairs


def _top_rows(x, n):
    sub = 8
    v = [x[i * sub:(i + 1) * sub] for i in range(x.shape[0] // sub)]
    size = 1 << (len(v) - 1).bit_length()
    v += [None] * (size - len(v))
    for i, j in _sort_network(size):
        if v[j] is None:
            continue
        if v[i] is None:
            v[i], v[j] = v[j], None
        else:
            v[i], v[j] = jnp.maximum(v[i], v[j]), jnp.minimum(v[i], v[j])
    v = [t for t in v if t is not None]
    rows = []
    for r in range(n):
        m = jnp.max(v[0], axis=0, keepdims=True)
        rows.append(m)
        left = n - r - 1
        if left:
            hit = v[0] == m
            for k in range(min(left, len(v))):
                nxt = v[k + 1] if k + 1 < len(v) else -jnp.inf
                v[k] = jnp.where(hit, nxt, v[k])
    return rows


def _rank_among(x, rows):
    rank = jnp.zeros(x.shape, F32)
    for q, row in enumerate(rows):
        rank = jnp.where(row > x, float(q + 1), rank)
    return rank


def _stack_rows(rows):
    n = len(rows)
    rid = lax.broadcasted_iota(jnp.int32, (n, LANES), 0)
    out = jnp.broadcast_to(rows[0], (n, LANES))
    for r in range(1, n):
        out = jnp.where(rid == r, rows[r], out)
    return out


def _peer_route_block(s1, s2):
    k = PEER_TOPK
    t1 = _top_rows(s1, k)
    t2 = _top_rows(s2, k)
    rank2 = _rank_among(s2, t2)
    t1s = _stack_rows(t1)
    t2s = _stack_rows(t2)
    half = k // 2
    rid = lax.broadcasted_iota(jnp.int32, (half, LANES), 0)
    cands = [t1s[0:half] + t2[0], t1s[half:k] + t2[0], t1s[0:half] + t2[1]]
    for q in range(2, half):
        cands.append(jnp.where(rid < k // (q + 1), t1s[0:half] + t2[q], -jnp.inf))
    cands.append(t2s[half:k] + t1[0])
    cand = jnp.concatenate(cands, axis=0)
    tau = _top_rows(cand, k)[k - 1]
    top = t1[0] + t2[0]
    z = jnp.sum(jnp.where(cand >= tau, jnp.exp(cand - top), 0.0), axis=0, keepdims=True)
    cnt = jnp.zeros(s1.shape, F32)
    for q in range(k):
        cnt = jnp.where(s1 + t2[q] >= tau, float(q + 1), cnt)
    a = jnp.exp(s1 - t1[0])
    b = jnp.exp(s2 - t2[0]) * (0.5 / z)
    return cnt, rank2, a, b


def _peer_kernel(x_ref, g_ref, wq_ref, keys_ref, u_ref, vt_ref, o_ref,
                 xn_ref, s1_ref, s2_ref, ca_ref, rb_ref, w_ref, acc_ref,
                 *, tt, ec):
    j = pl.program_id(1)
    n_chunks = pl.num_programs(1)
    nsb = tt // LANES
    c_per = ec // PEER_KEYS
    pk = BF16_SUBLANES

    n_pieces = nsb * PEER_KEYS // GATE_ROWS

    def gates(chunk, w_ref, pieces=range(n_pieces)):
        zero = jnp.zeros((pk, LANES), BF16)
        gpp = GATE_ROWS // pk
        ger = GATE_EXPERT_ROWS
        for cg in range(c_per // ger):
            c0 = pl.multiple_of(chunk * c_per + cg * ger, ger)
            for piece in pieces:
                sb = piece // (PEER_KEYS // GATE_ROWS)
                r0 = (piece % (PEER_KEYS // GATE_ROWS)) * GATE_ROWS
                w = [[zero] * gpp for _ in range(ger)]
                for h in range(PEER_HEADS):
                    cnt = [jnp.broadcast_to(ca_ref[h, sb, pl.ds(c0 + cl, 1), :],
                                            (pk, LANES)).astype(BF16) for cl in range(ger)]
                    a = [jnp.broadcast_to(ca_ref[h, sb, pl.ds(A_OFF + c0 + cl, 1), :],
                                          (pk, LANES)).astype(BF16) for cl in range(ger)]
                    for g in range(gpp):
                        lo = r0 + g * pk
                        rank = rb_ref[h, sb, lo:lo + pk, :]
                        b = rb_ref[h, sb, B_OFF + lo:B_OFF + lo + pk, :]
                        for cl in range(ger):
                            w[cl][g] = w[cl][g] + jnp.where(rank < cnt[cl], b * a[cl], zero)
                for cl in range(ger):
                    for g in range(gpp):
                        e0 = (cg * ger + cl) * PEER_KEYS + r0 + g * pk
                        w_ref[e0:e0 + pk, sb * LANES:(sb + 1) * LANES] = w[cl][g]

    @pl.when(j == 0)
    def _route():
        xn = _rmsnorm_rows(x_ref[...], g_ref[...]).astype(BF16)
        xn_ref[...] = xn

        def head_scores(h):
            qh = jnp.dot(xn_ref[...], wq_ref[h], preferred_element_type=F32)
            half = qh.shape[1] // 2
            for part, dst in ((0, s1_ref), (1, s2_ref)):
                qp = qh[:, part * half:(part + 1) * half].astype(BF16)
                st = lax.dot_general(keys_ref[h, part], qp, (((1,), (1,)), ((), ())),
                                     preferred_element_type=F32)
                for sb in range(nsb):
                    dst[h, sb] = st[:, sb * LANES:(sb + 1) * LANES]

        def route(idx):
            h, sb = idx // nsb, idx % nsb
            cnt, rank2, a, b = _peer_route_block(s1_ref[h, sb], s2_ref[h, sb])
            ca_ref[h, sb, 0:PEER_KEYS, :] = cnt
            ca_ref[h, sb, A_OFF:A_OFF + PEER_KEYS, :] = a
            rb_ref[h, sb, 0:PEER_KEYS, :] = rank2.astype(BF16)
            rb_ref[h, sb, B_OFF:B_OFF + PEER_KEYS, :] = b.astype(BF16)

        def per_trip(fn, n):
            def body(i, carry):
                for s in range(n):
                    fn(n * i + s)
                return carry
            return body

        lax.fori_loop(0, PEER_HEADS, per_trip(head_scores, 1), 0)
        lax.fori_loop(0, PEER_HEADS * nsb // 2, per_trip(route, 2), 0)
        acc_ref[...] = jnp.zeros(acc_ref.shape, F32)

    gates(j, w_ref)
    at = lax.dot_general(u_ref[...], xn_ref[...], (((1,), (1,)), ((), ())),
                         preferred_element_type=F32)
    p = w_ref[...] * _erf_gelu_x2(at).astype(BF16)
    acc_ref[...] += jnp.dot(vt_ref[...], p, preferred_element_type=F32)

    @pl.when(j == n_chunks - 1)
    def _finish():
        o_ref[...] = x_ref[...] + acc_ref[...].T


def _peer(x, g, wq_heads_bf, keys_bf, u_bf, vt_bf, tt, ec):
    t, d = x.shape
    n_exp = u_bf.shape[0]
    assert t % tt == 0 and n_exp % ec == 0 and tt % LANES == 0 and ec % PEER_KEYS == 0
    assert n_exp == PEER_KEYS * PEER_KEYS
    nsb = tt // LANES
    blk = lambda dt: pltpu.VMEM((PEER_HEADS, nsb, PEER_KEYS, LANES), dt)
    return pl.pallas_call(
        functools.partial(_peer_kernel, tt=tt, ec=ec),
        grid=(t // tt, n_exp // ec),
        in_specs=[pl.BlockSpec((tt, d), lambda i, j: (i, 0)),
                  pl.BlockSpec((1, d), lambda i, j: (0, 0)),
                  pl.BlockSpec(wq_heads_bf.shape, lambda i, j: (0, 0, 0)),
                  pl.BlockSpec(keys_bf.shape, lambda i, j: (0, 0, 0, 0)),
                  pl.BlockSpec((ec, d), lambda i, j: (j, 0)),
                  pl.BlockSpec((d, ec), lambda i, j: (0, j))],
        out_specs=pl.BlockSpec((tt, d), lambda i, j: (i, 0)),
        out_shape=jax.ShapeDtypeStruct((t, d), F32),
        scratch_shapes=[pltpu.VMEM((tt, d), BF16),
                        blk(F32), blk(F32),
                        pltpu.VMEM((PEER_HEADS, nsb, A_OFF + PEER_KEYS, LANES), F32),
                        pltpu.VMEM((PEER_HEADS, nsb, B_OFF + PEER_KEYS, LANES), BF16),
                        pltpu.VMEM((ec, tt), BF16),
                        pltpu.VMEM((d, tt), F32)],
        compiler_params=_cparams(("arbitrary", "arbitrary")),
        name="peer",
    )(x, g, wq_heads_bf, keys_bf, u_bf, vt_bf)


def _tail_kernel(x_ref, p_ref, gpl_ref, wg_ref, wpl_ref, gf_ref, o_ref):
    x = x_ref[...]
    xn = _rmsnorm_rows(x, gpl_ref[...]).astype(BF16)
    gate = jax.nn.sigmoid(jnp.dot(xn, wg_ref[...], preferred_element_type=F32))
    emb = jnp.dot(p_ref[...].astype(BF16), wpl_ref[...], preferred_element_type=F32)
    o_ref[...] = _rmsnorm_rows(x + gate * emb, gf_ref[...])


def _tail(x, p, g_pl, w_gate_bf, w_pl_bf, g_final, tm):
    t, d = x.shape
    pd = p.shape[1]
    return pl.pallas_call(
        _tail_kernel,
        grid=(t // tm,),
        in_specs=[pl.BlockSpec((tm, d), lambda i: (i, 0)),
                  pl.BlockSpec((tm, pd), lambda i: (i, 0)),
                  pl.BlockSpec((1, d), lambda i: (0, 0)),
                  pl.BlockSpec((d, d), lambda i: (0, 0)),
                  pl.BlockSpec((pd, d), lambda i: (0, 0)),
                  pl.BlockSpec((1, d), lambda i: (0, 0))],
        out_specs=pl.BlockSpec((tm, d), lambda i: (i, 0)),
        out_shape=jax.ShapeDtypeStruct((t, d), F32),
        compiler_params=_cparams(("arbitrary",)),
        name="tail",
    )(x, p, g_pl, w_gate_bf, w_pl_bf, g_final)


def _row_tile(t, want):
    tm = min(t, want)
    assert t % tm == 0
    return tm


def kernel(x_prompt, x_sample, cache_k, cache_v, page_table, state_pool, p_prompt, p_sample, norm_attn_g, w_in, lambda_q1, lambda_k1, lambda_q2, lambda_k2, subln_g, rel_bias, pool_w, pool_scale, w_out, norm_ffn_g, peer_wq, peer_keys, peer_u, peer_v, norm_pl_g, w_pl, w_pl_gate, final_norm_g):
    depth = w_in.shape[0]
    assert depth == 1
    batch, seq, d = x_prompt.shape
    nseq, dec, _ = x_sample.shape
    n_pages = page_table.shape[1]
    page = cache_k.shape[2]
    past_len = n_pages * page
    lam_init = 0.8 - 0.6 * math.exp(-0.3 * 0)
    i = 0

    row = lambda a: a.reshape(1, -1)
    w_in_bf = w_in[i].astype(BF16)
    pool_w_bf = pool_w[i].astype(BF16)
    w_out_bf = w_out[i].astype(BF16)
    qd = peer_wq.shape[2] // PEER_HEADS
    wq_heads_bf = peer_wq[i].reshape(d, PEER_HEADS, qd).transpose(1, 0, 2).astype(BF16)
    keys_bf = peer_keys[i].astype(BF16)
    u_bf = peer_u[i].astype(BF16)
    vt_bf = peer_v[i].astype(BF16).T
    w_gate_bf = w_pl_gate[i].astype(BF16)
    w_pl_bf = w_pl[i].astype(BF16)
    rb_flat = rel_bias.reshape(-1)
    lam_rows = (row(lambda_q1[i]), row(lambda_k1[i]), row(lambda_q2[i]), row(lambda_k2[i]))
    sg = row(subln_g[i])

    tp = batch * seq
    xp = x_prompt.reshape(tp, d)
    _, k, v, u, qb, kb, vtb = _inproj(xp, row(norm_attn_g[i]), w_in_bf, _row_tile(tp, 512))
    attn = _attn_prompt(qb, kb, vtb, rb_flat, *lam_rows, subln_g[i].reshape(-1, 1),
                        batch, seq, min(seq, 512), lam_init)
    x1 = _mix_prompt(xp, attn, u, pool_w_bf, row(pool_scale[i]), w_out_bf, seq, min(seq, 512))
    x2 = _peer(x1, row(norm_ffn_g[i]), wq_heads_bf, keys_bf, u_bf, vt_bf,
               _row_tile(tp, 512), min(u_bf.shape[0], PEER_EC))
    y_prompt = _tail(x2, p_prompt[i].reshape(tp, -1), row(norm_pl_g[i]), w_gate_bf, w_pl_bf,
                     row(final_norm_g), _row_tile(tp, 512)).reshape(batch, seq, d)
    new_k_prompt = k.reshape(1, batch, seq, HEADS, HEAD_W)
    new_v_prompt = v.reshape(1, batch, seq, HEADS, HEAD_W)
    new_pool_prompt = u.reshape(batch, seq, -1)[:, seq - POOL_PAD:, :][None]

    ts = nseq * dec
    xs = x_sample.reshape(ts, d)
    qs, ks, vs, us, _, _, _ = _inproj(xs, row(norm_attn_g[i]), w_in_bf, _row_tile(ts, 512))
    attn_s = _attn_sample(qs, ks, vs, cache_k, cache_v, i, page_table, rb_flat,
                          *lam_rows, sg, dec, lam_init, math.gcd(n_pages, 16))
    tok_major = lambda a: a.reshape(nseq, dec, -1).transpose(1, 0, 2)
    x1s = _mix_sample(tok_major(xs), tok_major(attn_s), tok_major(us),
                      state_pool[i].transpose(1, 0, 2), pool_w_bf, row(pool_scale[i]),
                      w_out_bf, past_len)
    x1s = x1s.transpose(1, 0, 2).reshape(ts, d)
    x2s = _peer(x1s, row(norm_ffn_g[i]), wq_heads_bf, keys_bf, u_bf, vt_bf,
                _row_tile(ts, 512), min(u_bf.shape[0], PEER_EC))
    y_sample = _tail(x2s, p_sample[i].reshape(ts, -1), row(norm_pl_g[i]), w_gate_bf, w_pl_bf,
                     row(final_norm_g), _row_tile(ts, 512)).reshape(nseq, dec, d)
    new_k_sample = ks.reshape(1, nseq, dec, HEADS, HEAD_W)
    new_v_sample = vs.reshape(1, nseq, dec, HEADS, HEAD_W)
    new_pool_sample = jnp.concatenate(
        [state_pool[i][:, dec:, :], us.reshape(nseq, dec, -1)], axis=1)[None]

    return (y_prompt, y_sample, new_k_prompt, new_v_prompt, new_pool_prompt,
            new_k_sample, new_v_sample, new_pool_sample)
```

```python
import functools
import math

import jax
import jax.numpy as jnp
import numpy as np
from jax import lax
from jax.experimental import pallas as pl
from jax.experimental.pallas import tpu as pltpu

F32 = jnp.float32
BF16 = jnp.bfloat16

EPS = 1e-6
NEG_INF = -1e30
LOG2E = 1.4426950408889634
HEADS = 4
QK_DIM = 64
HEAD_W = 2 * QK_DIM
ATT_W = HEADS * HEAD_W
POOL_WINDOWS = (2, 4, 8, 16)
POOL_GW = 128
POOL_PAD = 15
POOL_HALO = 16
REL_BUCKETS = 32
REL_MAX_DIST = 128
PEER_HEADS = 8
PEER_KEYS = 128
PEER_TOPK = 16
LANES = 128
BF16_SUBLANES = 16
GATE_ROWS = 128
GATE_EXPERT_ROWS = 4
PEER_EC = 2048
A_OFF = PEER_KEYS + 8
B_OFF = PEER_KEYS + BF16_SUBLANES
ONES_ROWS = 16
VMEM_LIMIT = 56 * 1024 * 1024


def _bucket_thresholds():
    exact = REL_BUCKETS // 2
    n = np.arange(0, 4 * REL_MAX_DIST)
    large = exact + (np.log(np.maximum(n, 1).astype(np.float32) / np.float32(exact))
                     / np.float32(math.log(REL_MAX_DIST / exact))
                     * np.float32(REL_BUCKETS - exact)).astype(np.int32)
    large = np.minimum(large, REL_BUCKETS - 1)
    bucket = np.where(n < exact, n, large)
    return [int(n[np.argmax(bucket >= b)]) for b in range(REL_BUCKETS)]


_BUCKET_START = _bucket_thresholds()
_FAR_DIST = _BUCKET_START[REL_BUCKETS - 1]


def _rel_bias_tile(rel, rb_ref, h):
    out = jnp.full(rel.shape, rb_ref[h], F32)
    for b in range(1, REL_BUCKETS):
        out = jnp.where(rel >= _BUCKET_START[b], rb_ref[b * HEADS + h], out)
    return out


def _rmsnorm_rows(x, g):
    return x * lax.rsqrt(jnp.mean(x * x, axis=-1, keepdims=True) + EPS) * g


def _cparams(sem):
    return pltpu.CompilerParams(dimension_semantics=sem, vmem_limit_bytes=VMEM_LIMIT)


def _inproj_kernel(x_ref, g_ref, w_ref, q_ref, k_ref, v_ref, u_ref, qb_ref, kb_ref, vtb_ref):
    xn = _rmsnorm_rows(x_ref[...], g_ref[...])
    proj = jnp.dot(xn.astype(BF16), w_ref[...], preferred_element_type=F32)
    q = proj[:, 0 * ATT_W:1 * ATT_W]
    k = proj[:, 1 * ATT_W:2 * ATT_W]
    v = proj[:, 2 * ATT_W:3 * ATT_W]
    q_ref[...] = q
    tm = q.shape[0]
    for h in range(HEADS):
        k_ref[pl.ds(h, tm, stride=HEADS), :] = k[:, h * HEAD_W:(h + 1) * HEAD_W]
        v_ref[pl.ds(h, tm, stride=HEADS), :] = v[:, h * HEAD_W:(h + 1) * HEAD_W]
    u_ref[...] = proj[:, 3 * ATT_W:4 * ATT_W]
    qb_ref[...] = (q * (QK_DIM ** -0.5 * LOG2E)).astype(BF16)
    kb_ref[...] = k.astype(BF16)
    vtb_ref[...] = v.T.astype(BF16)


def _inproj(x, g, w_bf, tm):
    t, d = x.shape
    n_out = w_bf.shape[1]
    assert n_out == 4 * ATT_W and t % tm == 0
    out = jax.ShapeDtypeStruct((t, ATT_W), F32)
    out_kv = jax.ShapeDtypeStruct((t * HEADS, HEAD_W), F32)
    out_bf = jax.ShapeDtypeStruct((t, ATT_W), BF16)
    row = pl.BlockSpec((tm, ATT_W), lambda i: (i, 0))
    row_kv = pl.BlockSpec((tm * HEADS, HEAD_W), lambda i: (i, 0))
    return pl.pallas_call(
        _inproj_kernel,
        grid=(t // tm,),
        in_specs=[pl.BlockSpec((tm, d), lambda i: (i, 0)),
                  pl.BlockSpec((1, d), lambda i: (0, 0)),
                  pl.BlockSpec((d, n_out), lambda i: (0, 0))],
        out_specs=[row, row_kv, row_kv, row, row, row, pl.BlockSpec((ATT_W, tm), lambda i: (0, i))],
        out_shape=[out, out_kv, out_kv, out, out_bf, out_bf,
                   jax.ShapeDtypeStruct((ATT_W, t), BF16)],
        compiler_params=_cparams(("arbitrary",)),
        name="inproj",
    )(x, g, w_bf)


def _lambda_full(lq1, lk1, lq2, lk2, lam_init):
    return (jnp.exp(jnp.sum(lq1 * lk1, axis=-1, keepdims=True))
            - jnp.exp(jnp.sum(lq2 * lk2, axis=-1, keepdims=True)) + lam_init)


def _head_out(acc1, l1, acc2, l2, lam, subln_g, lam_init):
    o = acc1 / l1 - lam * (acc2 / l2)
    return _rmsnorm_rows(o, subln_g) * (1.0 - lam_init)


def _attn_prompt_kernel(rb_ref, q_ref, k_ref, vt_ref, lq1_ref, lk1_ref, lq2_ref, lk2_ref, sg_ref,
                        o_ref, bias_ref, ve_ref, m_ref, acc_ref, *, tb, lam_init):
    b, qi, kj = pl.program_id(0), pl.program_id(1), pl.program_id(2)

    @pl.when((b == 0) & (qi == 0) & (kj == 0))
    def _build_constants():
        key = lax.broadcasted_iota(jnp.int32, (tb, tb), 0)
        qry = lax.broadcasted_iota(jnp.int32, (tb, tb), 1)
        rel = qry - key
        for h in range(HEADS):
            diag = _rel_bias_tile(rel, rb_ref, h) * LOG2E
            bias_ref[0, h] = jnp.where(rel >= 0, diag, NEG_INF)
            bias_ref[1, h] = _rel_bias_tile(rel + tb, rb_ref, h) * LOG2E
            ve_ref[h, HEAD_W:HEAD_W + ONES_ROWS, :] = jnp.ones((ONES_ROWS, tb), ve_ref.dtype)

    @pl.when(kj == 0)
    def _init():
        m_ref[...] = jnp.full(m_ref.shape, -jnp.inf, F32)
        acc_ref[...] = jnp.zeros(acc_ref.shape, F32)

    def step(near):
        for h in range(HEADS):
            ve_ref[h, 0:HEAD_W, :] = vt_ref[h * HEAD_W:(h + 1) * HEAD_W, :]
        for h in range(HEADS):
            for mp in range(2):
                c0 = h * HEAD_W + mp * QK_DIM
                idx = h * 2 + mp
                st = lax.dot_general(k_ref[:, c0:c0 + QK_DIM], q_ref[:, c0:c0 + QK_DIM],
                                     (((1,), (1,)), ((), ())), preferred_element_type=F32)
                m_old = m_ref[idx]
                if near:
                    st = st + bias_ref[jnp.where(kj == qi, 0, 1), h]
                    m_new = jnp.maximum(m_old, jnp.max(st, axis=0, keepdims=True))
                    shift = m_new
                else:
                    far = rb_ref[(REL_BUCKETS - 1) * HEADS + h] * LOG2E
                    m_new = jnp.maximum(m_old, jnp.max(st, axis=0, keepdims=True) + far)
                    shift = m_new - far
                alpha = jnp.exp2(m_old - m_new)
                p = jnp.exp2(st - shift).astype(BF16)
                acc_ref[idx] = alpha * acc_ref[idx] + jnp.dot(
                    ve_ref[h], p, preferred_element_type=F32)
                m_ref[idx] = m_new

    @pl.when((kj <= qi) & (kj >= qi - 1))
    def _near_step():
        step(True)

    @pl.when(kj < qi - 1)
    def _far_step():
        step(False)

    @pl.when(kj == qi)
    def _finish():
        lam = _lambda_full(lq1_ref[...], lk1_ref[...], lq2_ref[...], lk2_ref[...], lam_init)
        for h in range(HEADS):
            a1, a2 = acc_ref[2 * h], acc_ref[2 * h + 1]
            o = (a1[0:HEAD_W] / a1[HEAD_W:HEAD_W + 1]
                 - lam * (a2[0:HEAD_W] / a2[HEAD_W:HEAD_W + 1]))
            ms = jnp.mean(o * o, axis=0, keepdims=True)
            on = o * lax.rsqrt(ms + EPS) * sg_ref[...] * (1.0 - lam_init)
            o_ref[:, h * HEAD_W:(h + 1) * HEAD_W] = on.T


def _attn_prompt(q, k, vt, rb_flat, lq1, lk1, lq2, lk2, subln_col, batch, seq, tb, lam_init):
    t = q.shape[0]
    assert t == batch * seq and seq % tb == 0 and tb >= _FAR_DIST and q.dtype == BF16
    nq = seq // tb
    small = lambda n: pl.BlockSpec((1, n), lambda b, i, j: (0, 0))
    k_spec = pl.BlockSpec((tb, ATT_W), lambda b, i, j: (b * nq + jnp.minimum(j, i), 0))
    vt_spec = pl.BlockSpec((ATT_W, tb), lambda b, i, j: (0, b * nq + jnp.minimum(j, i)))
    q_spec = pl.BlockSpec((tb, ATT_W), lambda b, i, j: (b * nq + i, 0))
    return pl.pallas_call(
        functools.partial(_attn_prompt_kernel, tb=tb, lam_init=lam_init),
        grid=(batch, nq, nq),
        in_specs=[pl.BlockSpec(memory_space=pltpu.SMEM),
                  q_spec, k_spec, vt_spec,
                  small(QK_DIM), small(QK_DIM), small(QK_DIM), small(QK_DIM),
                  pl.BlockSpec((HEAD_W, 1), lambda b, i, j: (0, 0))],
        out_specs=q_spec,
        out_shape=jax.ShapeDtypeStruct((t, ATT_W), F32),
        scratch_shapes=[pltpu.VMEM((2, HEADS, tb, tb), F32),
                        pltpu.VMEM((HEADS, HEAD_W + ONES_ROWS, tb), BF16),
                        pltpu.VMEM((2 * HEADS, 1, tb), F32),
                        pltpu.VMEM((2 * HEADS, HEAD_W + ONES_ROWS, tb), F32)],
        compiler_params=_cparams(("arbitrary", "arbitrary", "arbitrary")),
        name="attn_prompt",
    )(rb_flat, q, k, vt, lq1, lk1, lq2, lk2, subln_col)


QROWS = 8


def _attn_sample_kernel(pt_ref, rb_ref, q_ref, kn_ref, vn_ref, *rest,
                        n_steps, pp, page, dec, lam_init):
    del pt_ref
    k_refs, v_refs = rest[:pp], rest[pp:2 * pp]
    (lq1_ref, lk1_ref, lq2_ref, lk2_ref, sg_ref, o_ref,
     q8_ref, qcat_ref, bias_ref, m_ref, l_ref, acc_ref) = rest[2 * pp:]
    b, p = pl.program_id(0), pl.program_id(1)
    hrows = 2 * QROWS
    rows = HEADS * hrows
    prow = page * HEADS
    width = pp * prow
    past_len = n_steps * pp * page
    row_i = lax.broadcasted_iota(jnp.int32, (rows, 1), 0) % QROWS

    def per_head(fn):
        return jnp.concatenate([fn(h) for h in range(HEADS)], axis=0)

    @pl.when(b == 0)
    def _build_bias():
        col = lax.broadcasted_iota(jnp.int32, (hrows, width), 1)
        k_pos = p * (pp * page) + col // HEADS
        rel = past_len + lax.broadcasted_iota(jnp.int32, (hrows, width), 0) % QROWS - k_pos
        bias_ref[p] = per_head(lambda h: jnp.where(col % HEADS == h,
                                                   _rel_bias_tile(rel, rb_ref, h), NEG_INF))

    @pl.when(p == 0)
    def _init():
        q8_ref[...] = jnp.zeros(q8_ref.shape, F32)
        q8_ref[0:dec, :] = q_ref[0] * (QK_DIM ** -0.5)
        col = lax.broadcasted_iota(jnp.int32, (QROWS, HEAD_W), 1)
        for h in range(HEADS):
            qh = q8_ref[:, h * HEAD_W:(h + 1) * HEAD_W]
            for mp in range(2):
                r0 = h * hrows + mp * QROWS
                qcat_ref[r0:r0 + QROWS, :] = jnp.where(col // QK_DIM == mp, qh, 0.0)
        m_ref[...] = jnp.full(m_ref.shape, -jnp.inf, F32)
        l_ref[...] = jnp.zeros(l_ref.shape, F32)
        acc_ref[...] = jnp.zeros(acc_ref.shape, F32)

    qcat = qcat_ref[...]
    qcat_bf = qcat.astype(BF16)
    s = jnp.concatenate(
        [lax.dot_general(qcat_bf, k_refs[r][...].astype(BF16), (((1,), (1,)), ((), ())),
                         preferred_element_type=F32) for r in range(pp)], axis=1)
    s = s + bias_ref[p]
    m_old = m_ref[...]
    m_new = jnp.maximum(m_old, jnp.max(s, axis=-1, keepdims=True))
    alpha = jnp.exp(m_old - m_new)
    pr = jnp.exp(s - m_new)
    l_ref[...] = alpha * l_ref[...] + jnp.sum(pr, axis=-1, keepdims=True)
    pv = jnp.dot(pr[:, 0:prow].astype(BF16), v_refs[0][...].astype(BF16),
                 preferred_element_type=F32)
    for r in range(1, pp):
        pv = pv + jnp.dot(pr[:, r * prow:(r + 1) * prow].astype(BF16),
                          v_refs[r][...].astype(BF16), preferred_element_type=F32)
    acc_ref[...] = alpha * acc_ref[...] + pv
    m_ref[...] = m_new

    @pl.when(p == n_steps - 1)
    def _finish():
        def head_row(ref, j, h):
            return ref[0, j * HEADS + h:j * HEADS + h + 1, :]

        s_new = []
        for j in range(dec):
            sj = per_head(lambda h, j=j: jnp.sum(
                qcat[h * hrows:(h + 1) * hrows] * head_row(kn_ref, j, h), axis=-1, keepdims=True)
                + _rel_bias_tile(row_i[0:hrows] - j, rb_ref, h))
            s_new.append(jnp.where(row_i >= j, sj, NEG_INF))
        m_old2 = m_ref[...]
        m_fin = m_old2
        for sj in s_new:
            m_fin = jnp.maximum(m_fin, sj)
        alpha2 = jnp.exp(m_old2 - m_fin)
        l_fin = alpha2 * l_ref[...]
        acc_fin = alpha2 * acc_ref[...]
        for j, sj in enumerate(s_new):
            pj = jnp.exp(sj - m_fin)
            l_fin = l_fin + pj
            acc_fin = acc_fin + pj * per_head(
                lambda h, j=j: jnp.broadcast_to(head_row(vn_ref, j, h), (hrows, HEAD_W)))
        lam = _lambda_full(lq1_ref[...], lk1_ref[...], lq2_ref[...], lk2_ref[...], lam_init)
        for h in range(HEADS):
            r1, r2 = h * hrows, h * hrows + QROWS
            oh = _head_out(acc_fin[r1:r1 + QROWS], l_fin[r1:r1 + QROWS],
                           acc_fin[r2:r2 + QROWS], l_fin[r2:r2 + QROWS],
                           lam, sg_ref[...], lam_init)
            o_ref[0, :, h * HEAD_W:(h + 1) * HEAD_W] = oh[0:dec]


def _attn_sample(q, k_new, v_new, cache_k, cache_v, layer, page_table, rb_flat,
                 lq1, lk1, lq2, lk2, subln_g, dec, lam_init, pp):
    nseq, n_pages = page_table.shape
    depth, n_phys, page = cache_k.shape[:3]
    assert dec <= QROWS and page >= _FAR_DIST and n_pages % pp == 0
    n_steps = n_pages // pp
    q3 = q.reshape(nseq, dec, ATT_W)
    kn3 = k_new.reshape(nseq, dec * HEADS, HEAD_W)
    vn3 = v_new.reshape(nseq, dec * HEADS, HEAD_W)
    new_spec = pl.BlockSpec((1, dec * HEADS, HEAD_W), lambda b, p, pt: (b, 0, 0))
    prow = page * HEADS
    kc = cache_k.reshape(depth * n_phys * prow, HEAD_W)
    vc = cache_v.reshape(depth * n_phys * prow, HEAD_W)
    rows = 2 * HEADS * QROWS
    seq_spec = pl.BlockSpec((1, dec, ATT_W), lambda b, p, pt: (b, 0, 0))

    def page_spec(r):
        return pl.BlockSpec((prow, HEAD_W),
                            lambda b, p, pt: (layer * n_phys + pt[b, p * pp + r], 0))

    pages = [page_spec(r) for r in range(pp)]
    small = lambda n: pl.BlockSpec((1, n), lambda b, p, pt: (0, 0))
    grid_spec = pltpu.PrefetchScalarGridSpec(
        num_scalar_prefetch=1,
        grid=(nseq, n_steps),
        in_specs=[pl.BlockSpec(memory_space=pltpu.SMEM), seq_spec, new_spec, new_spec]
        + pages + pages
        + [small(QK_DIM), small(QK_DIM), small(QK_DIM), small(QK_DIM), small(HEAD_W)],
        out_specs=seq_spec,
        scratch_shapes=[pltpu.VMEM((QROWS, ATT_W), F32),
                        pltpu.VMEM((rows, HEAD_W), F32),
                        pltpu.VMEM((n_steps, rows, pp * prow), F32),
                        pltpu.VMEM((rows, 1), F32),
                        pltpu.VMEM((rows, 1), F32),
                        pltpu.VMEM((rows, HEAD_W), F32)])
    out = pl.pallas_call(
        functools.partial(_attn_sample_kernel, n_steps=n_steps, pp=pp, page=page, dec=dec,
                          lam_init=lam_init),
        grid_spec=grid_spec,
        out_shape=jax.ShapeDtypeStruct((nseq, dec, ATT_W), F32),
        compiler_params=_cparams(("arbitrary", "arbitrary")),
        name="attn_sample",
    )(page_table, rb_flat, q3, kn3, vn3, *([kc] * pp), *([vc] * pp), lq1, lk1, lq2, lk2, subln_g)
    return out.reshape(nseq * dec, ATT_W)


def _pool_groups(window_sum, u_of, cnt_of, pw_ref, ps_ref):
    outs = []
    for g, w in enumerate(POOL_WINDOWS):
        d = window_sum(g, w) / cnt_of(w) - u_of(g)
        outs.append(jnp.dot(d.astype(BF16), pw_ref[g], preferred_element_type=F32))
    return jnp.concatenate(outs, axis=-1) * ps_ref[...]


def _mix_prompt_kernel(x_ref, a_ref, u_ref, halo_ref, pw_ref, ps_ref, wo_ref, o_ref, z_ref,
                       *, tm, tiles_per_seq):
    i = pl.program_id(0)
    t_in_seq = i % tiles_per_seq
    z_ref[0:POOL_HALO, :] = jnp.where(t_in_seq == 0, 0.0, halo_ref[...])
    z_ref[POOL_HALO:POOL_HALO + tm, :] = u_ref[...]
    pos = t_in_seq * tm + lax.broadcasted_iota(jnp.int32, (tm, 1), 0)

    def window_sum(g, w):
        cs = slice(g * POOL_GW, (g + 1) * POOL_GW)
        acc = z_ref[POOL_HALO:POOL_HALO + tm, cs]
        for k in range(1, w):
            acc = acc + z_ref[POOL_HALO - k:POOL_HALO - k + tm, cs]
        return acc

    pool = _pool_groups(
        window_sum,
        lambda g: u_ref[:, g * POOL_GW:(g + 1) * POOL_GW],
        lambda w: jnp.minimum(pos + 1, w).astype(F32),
        pw_ref, ps_ref)
    mixed = jnp.concatenate([a_ref[...], pool], axis=-1).astype(BF16)
    o_ref[...] = x_ref[...] + jnp.dot(mixed, wo_ref[...], preferred_element_type=F32)


def _mix_prompt(x, attn, u, pool_w_bf, pool_scale, w_out_bf, seq, tm):
    t, d = x.shape
    pw_cols = u.shape[1]
    assert seq % tm == 0 and tm % POOL_HALO == 0
    halo_blocks = tm // POOL_HALO
    return pl.pallas_call(
        functools.partial(_mix_prompt_kernel, tm=tm, tiles_per_seq=seq // tm),
        grid=(t // tm,),
        in_specs=[pl.BlockSpec((tm, d), lambda i: (i, 0)),
                  pl.BlockSpec((tm, ATT_W), lambda i: (i, 0)),
                  pl.BlockSpec((tm, pw_cols), lambda i: (i, 0)),
                  pl.BlockSpec((POOL_HALO, pw_cols),
                               lambda i: (jnp.maximum(i * halo_blocks - 1, 0), 0)),
                  pl.BlockSpec(pool_w_bf.shape, lambda i: (0, 0, 0)),
                  pl.BlockSpec((1, pw_cols), lambda i: (0, 0)),
                  pl.BlockSpec(w_out_bf.shape, lambda i: (0, 0))],
        out_specs=pl.BlockSpec((tm, d), lambda i: (i, 0)),
        out_shape=jax.ShapeDtypeStruct((t, d), F32),
        scratch_shapes=[pltpu.VMEM((POOL_HALO + tm, pw_cols), F32)],
        compiler_params=_cparams(("arbitrary",)),
        name="mix_prompt",
    )(x, attn, u, u, pool_w_bf, pool_scale, w_out_bf)


def _mix_sample_kernel(x_ref, a_ref, u_ref, st_ref, pw_ref, ps_ref, wo_ref, o_ref,
                       *, dec, start_pos):
    def z_row(r):
        return st_ref[r] if r < POOL_PAD else u_ref[r - POOL_PAD]

    for i in range(dec):
        def window_sum(g, w, i=i):
            cs = slice(g * POOL_GW, (g + 1) * POOL_GW)
            acc = z_row(POOL_PAD + i)[:, cs]
            for k in range(1, w):
                acc = acc + z_row(POOL_PAD + i - k)[:, cs]
            return acc

        pool = _pool_groups(
            window_sum,
            lambda g, i=i: u_ref[i][:, g * POOL_GW:(g + 1) * POOL_GW],
            lambda w, i=i: float(min(start_pos + i + 1, w)),
            pw_ref, ps_ref)
        mixed = jnp.concatenate([a_ref[i], pool], axis=-1).astype(BF16)
        o_ref[i] = x_ref[i] + jnp.dot(mixed, wo_ref[...], preferred_element_type=F32)


def _mix_sample(x3, attn3, u3, state3, pool_w_bf, pool_scale, w_out_bf, start_pos):
    dec, nseq, d = x3.shape
    full = lambda a: pl.BlockSpec(a.shape, lambda i: (0,) * a.ndim)
    args = (x3, attn3, u3, state3, pool_w_bf, pool_scale, w_out_bf)
    return pl.pallas_call(
        functools.partial(_mix_sample_kernel, dec=dec, start_pos=start_pos),
        grid=(1,),
        in_specs=[full(a) for a in args],
        out_specs=full(x3),
        out_shape=jax.ShapeDtypeStruct((dec, nseq, d), F32),
        compiler_params=_cparams(("arbitrary",)),
        name="mix_sample",
    )(*args)


def _erf_gelu_x2(x):
    return x + x * lax.erf(x * (2.0 ** -0.5))


def _sort_network(n):
    pairs = []

    def merge(lo, cnt, r):
        step = 2 * r
        if step < cnt:
            merge(lo, cnt, step)
            merge(lo + r, cnt, step)
            pairs.extend((i, i + r) for i in range(lo + r, lo + cnt - r, step))
        else:
            pairs.append((lo, lo + r))

    def sort(lo, cnt):
        if cnt > 1:
            sort(lo, cnt // 2)
            sort(lo + cnt // 2, cnt // 2)
            merge(lo, cnt, 1)

    sort(0, n)
    return pairs


def _top_rows(x, n):
    sub = 8
    v = [x[i * sub:(i + 1) * sub] for i in range(x.shape[0] // sub)]
    size = 1 << (len(v) - 1).bit_length()
    v += [None] * (size - len(v))
    for i, j in _sort_network(size):
        if v[j] is None:
            continue
        if v[i] is None:
            v[i], v[j] = v[j], None
        else:
            v[i], v[j] = jnp.maximum(v[i], v[j]), jnp.minimum(v[i], v[j])
    v = [t for t in v if t is not None]
    rows = []
    for r in range(n):
        m = jnp.max(v[0], axis=0, keepdims=True)
        rows.append(m)
        left = n - r - 1
        if left:
            hit = v[0] == m
            for k in range(min(left, len(v))):
                nxt = v[k + 1] if k + 1 < len(v) else -jnp.inf
                v[k] = jnp.where(hit, nxt, v[k])
    return rows


def _rank_among(x, rows):
    rank = jnp.zeros(x.shape, F32)
    for q, row in enumerate(rows):
        rank = jnp.where(row > x, float(q + 1), rank)
    return rank


def _stack_rows(rows):
    n = len(rows)
    rid = lax.broadcasted_iota(jnp.int32, (n, LANES), 0)
    out = jnp.broadcast_to(rows[0], (n, LANES))
    for r in range(1, n):
        out = jnp.where(rid == r, rows[r], out)
    return out


def _peer_route_block(s1, s2):
    k = PEER_TOPK
    t1 = _top_rows(s1, k)
    t2 = _top_rows(s2, k)
    rank2 = _rank_among(s2, t2)
    t1s = _stack_rows(t1)
    t2s = _stack_rows(t2)
    half = k // 2
    rid = lax.broadcasted_iota(jnp.int32, (half, LANES), 0)
    cands = [t1s[0:half] + t2[0], t1s[half:k] + t2[0], t1s[0:half] + t2[1]]
    for q in range(2, half):
        cands.append(jnp.where(rid < k // (q + 1), t1s[0:half] + t2[q], -jnp.inf))
    cands.append(t2s[half:k] + t1[0])
    cand = jnp.concatenate(cands, axis=0)
    tau = _top_rows(cand, k)[k - 1]
    top = t1[0] + t2[0]
    z = jnp.sum(jnp.where(cand >= tau, jnp.exp(cand - top), 0.0), axis=0, keepdims=True)
    cnt = jnp.zeros(s1.shape, F32)
    for q in range(k):
        cnt = jnp.where(s1 + t2[q] >= tau, float(q + 1), cnt)
    a = jnp.exp(s1 - t1[0])
    b = jnp.exp(s2 - t2[0]) * (0.5 / z)
    return cnt, rank2, a, b


def _peer_kernel(x_ref, g_ref, wq_ref, keys_ref, u_ref, vt_ref, o_ref,
                 xn_ref, s1_ref, s2_ref, ca_ref, rb_ref, w_ref, acc_ref,
                 *, tt, ec):
    j = pl.program_id(1)
    n_chunks = pl.num_programs(1)
    nsb = tt // LANES
    c_per = ec // PEER_KEYS
    pk = BF16_SUBLANES

    n_pieces = nsb * PEER_KEYS // GATE_ROWS

    def gates(chunk, w_ref, pieces=range(n_pieces)):
        zero = jnp.zeros((pk, LANES), BF16)
        gpp = GATE_ROWS // pk
        ger = GATE_EXPERT_ROWS
        for cg in range(c_per // ger):
            c0 = pl.multiple_of(chunk * c_per + cg * ger, ger)
            for piece in pieces:
                sb = piece // (PEER_KEYS // GATE_ROWS)
                r0 = (piece % (PEER_KEYS // GATE_ROWS)) * GATE_ROWS
                w = [[zero] * gpp for _ in range(ger)]
                for h in range(PEER_HEADS):
                    cnt = [jnp.broadcast_to(ca_ref[h, sb, pl.ds(c0 + cl, 1), :],
                                            (pk, LANES)).astype(BF16) for cl in range(ger)]
                    a = [jnp.broadcast_to(ca_ref[h, sb, pl.ds(A_OFF + c0 + cl, 1), :],
                                          (pk, LANES)).astype(BF16) for cl in range(ger)]
                    for g in range(gpp):
                        lo = r0 + g * pk
                        rank = rb_ref[h, sb, lo:lo + pk, :]
                        b = rb_ref[h, sb, B_OFF + lo:B_OFF + lo + pk, :]
                        for cl in range(ger):
                            w[cl][g] = w[cl][g] + jnp.where(rank < cnt[cl], b * a[cl], zero)
                for cl in range(ger):
                    for g in range(gpp):
                        e0 = (cg * ger + cl) * PEER_KEYS + r0 + g * pk
                        w_ref[e0:e0 + pk, sb * LANES:(sb + 1) * LANES] = w[cl][g]

    @pl.when(j == 0)
    def _route():
        xn = _rmsnorm_rows(x_ref[...], g_ref[...]).astype(BF16)
        xn_ref[...] = xn

        def head_scores(h):
            qh = jnp.dot(xn_ref[...], wq_ref[h], preferred_element_type=F32)
            half = qh.shape[1] // 2
            for part, dst in ((0, s1_ref), (1, s2_ref)):
                qp = qh[:, part * half:(part + 1) * half].astype(BF16)
                st = lax.dot_general(keys_ref[h, part], qp, (((1,), (1,)), ((), ())),
                                     preferred_element_type=F32)
                for sb in range(nsb):
                    dst[h, sb] = st[:, sb * LANES:(sb + 1) * LANES]

        def route(idx):
            h, sb = idx // nsb, idx % nsb
            cnt, rank2, a, b = _peer_route_block(s1_ref[h, sb], s2_ref[h, sb])
            ca_ref[h, sb, 0:PEER_KEYS, :] = cnt
            ca_ref[h, sb, A_OFF:A_OFF + PEER_KEYS, :] = a
            rb_ref[h, sb, 0:PEER_KEYS, :] = rank2.astype(BF16)
            rb_ref[h, sb, B_OFF:B_OFF + PEER_KEYS, :] = b.astype(BF16)

        def per_trip(fn, n):
            def body(i, carry):
                for s in range(n):
                    fn(n * i + s)
                return carry
            return body

        lax.fori_loop(0, PEER_HEADS, per_trip(head_scores, 1), 0)
        lax.fori_loop(0, PEER_HEADS * nsb // 2, per_trip(route, 2), 0)
        acc_ref[...] = jnp.zeros(acc_ref.shape, F32)

    gates(j, w_ref)
    at = lax.dot_general(u_ref[...], xn_ref[...], (((1,), (1,)), ((), ())),
                         preferred_element_type=F32)
    p = w_ref[...] * _erf_gelu_x2(at).astype(BF16)
    acc_ref[...] += jnp.dot(vt_ref[...], p, preferred_element_type=F32)

    @pl.when(j == n_chunks - 1)
    def _finish():
        o_ref[...] = x_ref[...] + acc_ref[...].T


def _peer(x, g, wq_heads_bf, keys_bf, u_bf, vt_bf, tt, ec):
    t, d = x.shape
    n_exp = u_bf.shape[0]
    assert t % tt == 0 and n_exp % ec == 0 and tt % LANES == 0 and ec % PEER_KEYS == 0
    assert n_exp == PEER_KEYS * PEER_KEYS
    nsb = tt // LANES
    blk = lambda dt: pltpu.VMEM((PEER_HEADS, nsb, PEER_KEYS, LANES), dt)
    return pl.pallas_call(
        functools.partial(_peer_kernel, tt=tt, ec=ec),
        grid=(t // tt, n_exp // ec),
        in_specs=[pl.BlockSpec((tt, d), lambda i, j: (i, 0)),
                  pl.BlockSpec((1, d), lambda i, j: (0, 0)),
                  pl.BlockSpec(wq_heads_bf.shape, lambda i, j: (0, 0, 0)),
                  pl.BlockSpec(keys_bf.shape, lambda i, j: (0, 0, 0, 0)),
                  pl.BlockSpec((ec, d), lambda i, j: (j, 0)),
                  pl.BlockSpec((d, ec), lambda i, j: (0, j))],
        out_specs=pl.BlockSpec((tt, d), lambda i, j: (i, 0)),
        out_shape=jax.ShapeDtypeStruct((t, d), F32),
        scratch_shapes=[pltpu.VMEM((tt, d), BF16),
                        blk(F32), blk(F32),
                        pltpu.VMEM((PEER_HEADS, nsb, A_OFF + PEER_KEYS, LANES), F32),
                        pltpu.VMEM((PEER_HEADS, nsb, B_OFF + PEER_KEYS, LANES), BF16),
                        pltpu.VMEM((ec, tt), BF16),
                        pltpu.VMEM((d, tt), F32)],
        compiler_params=_cparams(("arbitrary", "arbitrary")),
        name="peer",
    )(x, g, wq_heads_bf, keys_bf, u_bf, vt_bf)


def _tail_kernel(x_ref, p_ref, gpl_ref, wg_ref, wpl_ref, gf_ref, o_ref):
    x = x_ref[...]
    xn = _rmsnorm_rows(x, gpl_ref[...]).astype(BF16)
    gate = jax.nn.sigmoid(jnp.dot(xn, wg_ref[...], preferred_element_type=F32))
    emb = jnp.dot(p_ref[...].astype(BF16), wpl_ref[...], preferred_element_type=F32)
    o_ref[...] = _rmsnorm_rows(x + gate * emb, gf_ref[...])


def _tail(x, p, g_pl, w_gate_bf, w_pl_bf, g_final, tm):
    t, d = x.shape
    pd = p.shape[1]
    return pl.pallas_call(
        _tail_kernel,
        grid=(t // tm,),
        in_specs=[pl.BlockSpec((tm, d), lambda i: (i, 0)),
                  pl.BlockSpec((tm, pd), lambda i: (i, 0)),
                  pl.BlockSpec((1, d), lambda i: (0, 0)),
                  pl.BlockSpec((d, d), lambda i: (0, 0)),
                  pl.BlockSpec((pd, d), lambda i: (0, 0)),
                  pl.BlockSpec((1, d), lambda i: (0, 0))],
        out_specs=pl.BlockSpec((tm, d), lambda i: (i, 0)),
        out_shape=jax.ShapeDtypeStruct((t, d), F32),
        compiler_params=_cparams(("arbitrary",)),
        name="tail",
    )(x, p, g_pl, w_gate_bf, w_pl_bf, g_final)


def _row_tile(t, want):
    tm = min(t, want)
    assert t % tm == 0
    return tm


def kernel(x_prompt, x_sample, cache_k, cache_v, page_table, state_pool, p_prompt, p_sample, norm_attn_g, w_in, lambda_q1, lambda_k1, lambda_q2, lambda_k2, subln_g, rel_bias, pool_w, pool_scale, w_out, norm_ffn_g, peer_wq, peer_keys, peer_u, peer_v, norm_pl_g, w_pl, w_pl_gate, final_norm_g):
    depth = w_in.shape[0]
    assert depth == 1
    batch, seq, d = x_prompt.shape
    nseq, dec, _ = x_sample.shape
    n_pages = page_table.shape[1]
    page = cache_k.shape[2]
    past_len = n_pages * page
    lam_init = 0.8 - 0.6 * math.exp(-0.3 * 0)
    i = 0

    row = lambda a: a.reshape(1, -1)
    w_in_bf = w_in[i].astype(BF16)
    pool_w_bf = pool_w[i].astype(BF16)
    w_out_bf = w_out[i].astype(BF16)
    qd = peer_wq.shape[2] // PEER_HEADS
    wq_heads_bf = peer_wq[i].reshape(d, PEER_HEADS, qd).transpose(1, 0, 2).astype(BF16)
    keys_bf = peer_keys[i].astype(BF16)
    u_bf = peer_u[i].astype(BF16)
    vt_bf = peer_v[i].astype(BF16).T
    w_gate_bf = w_pl_gate[i].astype(BF16)
    w_pl_bf = w_pl[i].astype(BF16)
    rb_flat = rel_bias.reshape(-1)
    lam_rows = (row(lambda_q1[i]), row(lambda_k1[i]), row(lambda_q2[i]), row(lambda_k2[i]))
    sg = row(subln_g[i])

    tp = batch * seq
    xp = x_prompt.reshape(tp, d)
    _, k, v, u, qb, kb, vtb = _inproj(xp, row(norm_attn_g[i]), w_in_bf, _row_tile(tp, 512))
    attn = _attn_prompt(qb, kb, vtb, rb_flat, *lam_rows, subln_g[i].reshape(-1, 1),
                        batch, seq, min(seq, 512), lam_init)
    x1 = _mix_prompt(xp, attn, u, pool_w_bf, row(pool_scale[i]), w_out_bf, seq, min(seq, 512))
    x2 = _peer(x1, row(norm_ffn_g[i]), wq_heads_bf, keys_bf, u_bf, vt_bf,
               _row_tile(tp, 512), min(u_bf.shape[0], PEER_EC))
    y_prompt = _tail(x2, p_prompt[i].reshape(tp, -1), row(norm_pl_g[i]), w_gate_bf, w_pl_bf,
                     row(final_norm_g), _row_tile(tp, 512)).reshape(batch, seq, d)
    new_k_prompt = k.reshape(1, batch, seq, HEADS, HEAD_W)
    new_v_prompt = v.reshape(1, batch, seq, HEADS, HEAD_W)
    new_pool_prompt = u.reshape(batch, seq, -1)[:, seq - POOL_PAD:, :][None]

    ts = nseq * dec
    xs = x_sample.reshape(ts, d)
    qs, ks, vs, us, _, _, _ = _inproj(xs, row(norm_attn_g[i]), w_in_bf, _row_tile(ts, 512))
    attn_s = _attn_sample(qs, ks, vs, cache_k, cache_v, i, page_table, rb_flat,
                          *lam_rows, sg, dec, lam_init, math.gcd(n_pages, 16))
    tok_major = lambda a: a.reshape(nseq, dec, -1).transpose(1, 0, 2)
    x1s = _mix_sample(tok_major(xs), tok_major(attn_s), tok_major(us),
                      state_pool[i].transpose(1, 0, 2), pool_w_bf, row(pool_scale[i]),
                      w_out_bf, past_len)
    x1s = x1s.transpose(1, 0, 2).reshape(ts, d)
    x2s = _peer(x1s, row(norm_ffn_g[i]), wq_heads_bf, keys_bf, u_bf, vt_bf,
                _row_tile(ts, 512), min(u_bf.shape[0], PEER_EC))
    y_sample = _tail(x2s, p_sample[i].reshape(ts, -1), row(norm_pl_g[i]), w_gate_bf, w_pl_bf,
                     row(final_norm_g), _row_tile(ts, 512)).reshape(nseq, dec, d)
    new_k_sample = ks.reshape(1, nseq, dec, HEADS, HEAD_W)
    new_v_sample = vs.reshape(1, nseq, dec, HEADS, HEAD_W)
    new_pool_sample = jnp.concatenate(
        [state_pool[i][:, dec:, :], us.reshape(nseq, dec, -1)], axis=1)[None]

    return (y_prompt, y_sample, new_k_prompt, new_v_prompt, new_pool_prompt,
            new_k_sample, new_v_sample, new_pool_sample)
```
